```python
import math
import jax, jax.numpy as jnp
from jax import lax
import numpy as np

D_MODEL = 2048
BATCH = 1
SEQ = 8192
DEPTH = 1

MIX_WIDTH = D_MODEL
SSM_WIDTH = MIX_WIDTH // 2
SSM_GROUP = 16
SSM_GROUPS = SSM_WIDTH // SSM_GROUP
SSM_STATE = 64
SGU_WIDTH = MIX_WIDTH - SSM_WIDTH
SGU_HEADS = 8
SGU_HEAD_DIM = SGU_WIDTH // SGU_HEADS
SGU_CHUNK = 128
IN_WIDTH = SSM_WIDTH + 2 * SGU_WIDTH
D_FF = 4 * D_MODEL
EPS = 1e-6
DT_MIN = 1e-3
DT_MAX = 1e-1

kernel_name = "hymba_style_s5_sgu_hybrid_layer"


def rmsnorm(x, g):
    xf = x.astype(jnp.float32)
    xf = xf * lax.rsqrt(jnp.mean(xf * xf, axis=-1, keepdims=True) + EPS)
    return (xf * g.astype(jnp.float32)).astype(x.dtype)


def layernorm(x, g, b):
    xf = x.astype(jnp.float32)
    mu = jnp.mean(xf, axis=-1, keepdims=True)
    var = jnp.mean(jnp.square(xf - mu), axis=-1, keepdims=True)
    y = (xf - mu) * lax.rsqrt(var + EPS) * g.astype(jnp.float32) + b.astype(jnp.float32)
    return y.astype(x.dtype)


def _scan_combine(left, right):
    a_l, b_l = left
    a_r, b_r = right
    return a_l * a_r, a_r * b_l + b_r


def s5_mixer(u, a_re, a_im, b_re, b_im, c_re, c_im, d, log_dt, glu_w, glu_b):
    dtype = u.dtype
    bsz, seq, _ = u.shape
    uf = u.astype(jnp.float32).reshape(bsz, seq, SSM_GROUPS, SSM_GROUP)
    lam = lax.complex(a_re.astype(jnp.float32), a_im.astype(jnp.float32))
    dt = jnp.exp(log_dt.astype(jnp.float32))[:, None]
    a_bar = jnp.exp(lam * dt)
    b_mat = lax.complex(b_re.astype(jnp.float32), b_im.astype(jnp.float32))
    b_bar = ((a_bar - 1.0) / lam)[..., None] * b_mat
    bu = jnp.einsum('bsgh,gph->bsgp', uf.astype(jnp.complex64), b_bar)
    a_seq = jnp.broadcast_to(a_bar, bu.shape)
    _, states = lax.associative_scan(_scan_combine, (a_seq, bu), axis=1)
    c_mat = lax.complex(c_re.astype(jnp.float32), c_im.astype(jnp.float32))
    y = jnp.einsum('bsgp,ghp->bsgh', states, c_mat).real + d.astype(jnp.float32) * uf
    y = jax.nn.gelu(y.reshape(bsz, seq, SSM_WIDTH))
    gate = jax.nn.sigmoid(y @ glu_w.astype(jnp.float32) + glu_b.astype(jnp.float32))
    return (y * gate).astype(dtype)


def sgu_mixer(z, ln_g, ln_b, w_s, b_s):
    bsz, seq, _ = z.shape
    u, v = jnp.split(jax.nn.gelu(z), 2, axis=-1)
    v = layernorm(v, ln_g, ln_b)
    v = v.reshape(bsz, seq // SGU_CHUNK, SGU_CHUNK, SGU_HEADS, SGU_HEAD_DIM)
    causal = jnp.tril(jnp.ones((SGU_CHUNK, SGU_CHUNK), dtype=bool))
    w = jnp.where(causal[None], w_s, jnp.zeros_like(w_s))
    mixed = jnp.einsum('hts,bcshd->bcthd', w, v) + b_s.T[:, :, None]
    return u * mixed.reshape(bsz, seq, SGU_WIDTH)


def setup_inputs(seed: int = 0) -> dict:
    key = jax.random.key(seed)
    ks = jax.random.split(key, 24)
    L = DEPTH
    G, P, H = SSM_GROUPS, SSM_STATE, SSM_GROUP
    f32 = jnp.float32
    nrm = lambda k, shape: jax.random.normal(k, shape, f32)
    x = nrm(ks[0], (BATCH, SEQ, D_MODEL))
    norm_mix_g = 1.0 + 0.02 * nrm(ks[1], (L, D_MODEL))
    w_in = nrm(ks[2], (L, D_MODEL, IN_WIDTH)) * D_MODEL ** -0.5
    n = jnp.arange(P, dtype=f32)
    ssm_a_re = -0.5 + 0.01 * nrm(ks[3], (L, G, P))
    ssm_a_im = math.pi * n + 0.01 * nrm(ks[4], (L, G, P))
    ssm_b_re = nrm(ks[5], (L, G, P, H)) * (2.0 * H) ** -0.5
    ssm_b_im = nrm(ks[6], (L, G, P, H)) * (2.0 * H) ** -0.5
    ssm_c_re = nrm(ks[7], (L, G, H, P)) * (2.0 * P) ** -0.5
    ssm_c_im = nrm(ks[8], (L, G, H, P)) * (2.0 * P) ** -0.5
    ssm_d = nrm(ks[9], (L, G, H))
    ssm_log_dt = jax.random.uniform(ks[10], (L, G), f32, math.log(DT_MIN), math.log(DT_MAX))
    ssm_glu_w = nrm(ks[11], (L, SSM_WIDTH, SSM_WIDTH)) * SSM_WIDTH ** -0.5
    ssm_glu_b = 0.01 * nrm(ks[12], (L, SSM_WIDTH))
    sgu_ln_g = 1.0 + 0.02 * nrm(ks[13], (L, SGU_WIDTH))
    sgu_ln_b = 0.01 * nrm(ks[14], (L, SGU_WIDTH))
    sgu_w = nrm(ks[15], (L, SGU_HEADS, SGU_CHUNK, SGU_CHUNK)) * 0.5 * SGU_CHUNK ** -0.5
    sgu_b = 1.0 + 0.01 * nrm(ks[16], (L, SGU_HEADS, SGU_CHUNK))
    out_norm_ssm_g = 1.0 + 0.02 * nrm(ks[17], (L, SSM_WIDTH))
    out_norm_sgu_g = 1.0 + 0.02 * nrm(ks[18], (L, SGU_WIDTH))
    w_out = nrm(ks[19], (L, MIX_WIDTH, D_MODEL)) * MIX_WIDTH ** -0.5
    norm_mlp_g = 1.0 + 0.02 * nrm(ks[20], (L, D_MODEL))
    w_up = nrm(ks[21], (L, D_MODEL, D_FF)) * D_MODEL ** -0.5
    w_down = nrm(ks[22], (L, D_FF, D_MODEL)) * D_FF ** -0.5
    norm_final_g = 1.0 + 0.02 * nrm(ks[23], (D_MODEL,))
    return {"x": x, "norm_mix_g": norm_mix_g, "w_in": w_in,
            "ssm_a_re": ssm_a_re, "ssm_a_im": ssm_a_im,
            "ssm_b_re": ssm_b_re, "ssm_b_im": ssm_b_im,
            "ssm_c_re": ssm_c_re, "ssm_c_im": ssm_c_im,
            "ssm_d": ssm_d, "ssm_log_dt": ssm_log_dt,
            "ssm_glu_w": ssm_glu_w, "ssm_glu_b": ssm_glu_b,
            "sgu_ln_g": sgu_ln_g, "sgu_ln_b": sgu_ln_b,
            "sgu_w": sgu_w, "sgu_b": sgu_b,
            "out_norm_ssm_g": out_norm_ssm_g, "out_norm_sgu_g": out_norm_sgu_g,
            "w_out": w_out, "norm_mlp_g": norm_mlp_g,
            "w_up": w_up, "w_down": w_down, "norm_final_g": norm_final_g}


def reference(x, norm_mix_g, w_in, ssm_a_re, ssm_a_im, ssm_b_re, ssm_b_im,
              ssm_c_re, ssm_c_im, ssm_d, ssm_log_dt, ssm_glu_w, ssm_glu_b,
              sgu_ln_g, sgu_ln_b, sgu_w, sgu_b, out_norm_ssm_g, out_norm_sgu_g,
              w_out, norm_mlp_g, w_up, w_down, norm_final_g):
    for l in range(DEPTH):
        h = rmsnorm(x, norm_mix_g[l])
        z = h @ w_in[l]
        z_ssm = z[..., :SSM_WIDTH]
        z_sgu = z[..., SSM_WIDTH:]
        y_ssm = s5_mixer(z_ssm, ssm_a_re[l], ssm_a_im[l], ssm_b_re[l], ssm_b_im[l],
                         ssm_c_re[l], ssm_c_im[l], ssm_d[l], ssm_log_dt[l],
                         ssm_glu_w[l], ssm_glu_b[l])
        y_sgu = sgu_mixer(z_sgu, sgu_ln_g[l], sgu_ln_b[l], sgu_w[l], sgu_b[l])
        mixed = jnp.concatenate([rmsnorm(y_ssm, out_norm_ssm_g[l]),
                                 rmsnorm(y_sgu, out_norm_sgu_g[l])], axis=-1)
        x = x + mixed @ w_out[l]
        h = rmsnorm(x, norm_mlp_g[l])
        x = x + jnp.square(jax.nn.relu(h @ w_up[l])) @ w_down[l]
    return rmsnorm(x, norm_final_g)
```

```python
import functools

import jax
import jax.numpy as jnp
from jax import lax
from jax.experimental import pallas as pl
from jax.experimental.pallas import tpu as pltpu

D_MODEL = 2048
SEQ = 8192
SSM_WIDTH = 1024
SSM_GROUP = 16
SSM_STATE = 64
SGU_WIDTH = 1024
SGU_HEADS = 8
SGU_HEAD_DIM = 128
SGU_CHUNK = 128
IN_WIDTH = SSM_WIDTH + 2 * SGU_WIDTH
D_FF = 4 * D_MODEL
EPS = 1e-6

LANES = 128
S5_T = 16
S5_CHUNKS = SEQ // S5_T
S5_BLOCKS = SSM_WIDTH // LANES
S5_GROUPS_PER_BLOCK = LANES // SSM_GROUP
S5_BLOCK_STATE = S5_GROUPS_PER_BLOCK * SSM_STATE
S5_CHUNK_COLS = S5_T * LANES

ROW_TILE = 512
MLP_ROW_TILE = 1024
MLP_FF_TILE = 512
VMEM_LIMIT = 56 * 1024 * 1024

F32 = jnp.float32
BF16 = jnp.bfloat16


def _rms(x, g):
    return x * lax.rsqrt(jnp.mean(x * x, axis=-1, keepdims=True) + EPS) * g


def _const_spec(shape):
    n = len(shape)
    return pl.BlockSpec(shape, lambda *_: (0,) * n, pipeline_mode=pl.Buffered(1))


def _in_proj_kernel(x_ref, g_ref, win_ref, lng_ref, lnb_ref, sw_ref, sbias_ref,
                    og_ref, u_ref, sgu_ref, y_scr):
    tm = x_ref.shape[0]
    h = _rms(x_ref[...], g_ref[...]).astype(BF16)
    z = jnp.dot(h, win_ref[...], preferred_element_type=F32)

    zs = z[:, :SSM_WIDTH].astype(BF16)
    for b in range(S5_BLOCKS):
        u_ref[b] = zs[:, b * LANES:(b + 1) * LANES].reshape(tm // S5_T, S5_T, LANES)

    a = jax.nn.gelu(z[:, SSM_WIDTH:])
    u = a[:, :SGU_WIDTH]
    v = a[:, SGU_WIDTH:]
    mu = jnp.mean(v, axis=-1, keepdims=True)
    vc = v - mu
    var = jnp.mean(vc * vc, axis=-1, keepdims=True)
    vb = (vc * lax.rsqrt(var + EPS) * lng_ref[...] + lnb_ref[...]).astype(BF16)

    row = lax.broadcasted_iota(jnp.int32, (SGU_CHUNK, SGU_CHUNK), 0)
    col = lax.broadcasted_iota(jnp.int32, (SGU_CHUNK, SGU_CHUNK), 1)
    causal = row >= col
    for hd in range(SGU_HEADS):
        w = jnp.where(causal, sw_ref[hd], 0.0).astype(BF16)
        cs = slice(hd * SGU_HEAD_DIM, (hd + 1) * SGU_HEAD_DIM)
        for ck in range(tm // SGU_CHUNK):
            rs = slice(ck * SGU_CHUNK, (ck + 1) * SGU_CHUNK)
            mixed = jnp.dot(w, vb[rs, cs], preferred_element_type=F32)
            y_scr[rs, cs] = u[rs, cs] * (mixed + sbias_ref[:, cs])
    sgu_ref[...] = _rms(y_scr[...], og_ref[...]).astype(BF16)


def _in_proj(x, g, w_in, ln_g, ln_b, sgu_w, sgu_bias, og):
    tm = ROW_TILE
    return pl.pallas_call(
        _in_proj_kernel,
        grid=(SEQ // tm,),
        in_specs=[
            pl.BlockSpec((tm, D_MODEL), lambda i: (i, 0)),
            _const_spec((1, D_MODEL)),
            _const_spec((D_MODEL, IN_WIDTH)),
            _const_spec((1, SGU_WIDTH)),
            _const_spec((1, SGU_WIDTH)),
            _const_spec((SGU_HEADS, SGU_CHUNK, SGU_CHUNK)),
            _const_spec((SGU_CHUNK, SGU_WIDTH)),
            _const_spec((1, SGU_WIDTH)),
        ],
        out_specs=[
            pl.BlockSpec((S5_BLOCKS, tm // S5_T, S5_T, LANES), lambda i: (0, i, 0, 0)),
            pl.BlockSpec((tm, SGU_WIDTH), lambda i: (i, 0)),
        ],
        out_shape=[
            jax.ShapeDtypeStruct((S5_BLOCKS, S5_CHUNKS, S5_T, LANES), BF16),
            jax.ShapeDtypeStruct((SEQ, SGU_WIDTH), BF16),
        ],
        scratch_shapes=[pltpu.VMEM((tm, SGU_WIDTH), F32)],
        compiler_params=pltpu.CompilerParams(
            dimension_semantics=("arbitrary",), vmem_limit_bytes=VMEM_LIMIT),
        name="in_proj_sgu",
    )(x, g, w_in, ln_g, ln_b, sgu_w, sgu_bias, og)


def _split_bf16(x):
    hi = x.astype(BF16)
    lo = (x - hi.astype(F32)).astype(BF16)
    return hi, lo


def _s5_kernel(u_ref, rowp_ref, colp_ref, btr_ref, bti_ref, ctr_ref, cti_ref, d_ref,
               y_ref, bx_scr, cy_scr, k2_scr, x_scr):
    nst = S5_BLOCK_STATE

    are = rowp_ref[0, 0:1, :]
    aim = rowp_ref[0, 1:2, :]
    dt = jnp.exp(rowp_ref[0, 2:3, :])
    er = jnp.exp(are * dt)
    abr = er * jnp.cos(aim * dt)
    abi = er * jnp.sin(aim * dt)
    den = are * are + aim * aim
    cr = ((abr - 1.0) * are + abi * aim) / den
    ci = (abi * are - (abr - 1.0) * aim) / den

    grow = lax.broadcasted_iota(jnp.int32, (LANES, nst), 0) >> 4
    gcol = lax.broadcasted_iota(jnp.int32, (LANES, nst), 1) >> 6
    same = grow == gcol
    btr = jnp.concatenate([btr_ref[0]] * S5_GROUPS_PER_BLOCK, axis=0)
    bti = jnp.concatenate([bti_ref[0]] * S5_GROUPS_PER_BLOCK, axis=0)
    bbr = jnp.where(same, cr * btr - ci * bti, 0.0)
    bbi = jnp.where(same, cr * bti + ci * btr, 0.0)

    pr = jnp.ones_like(abr)
    pi = jnp.zeros_like(abr)
    for k in range(S5_T):
        rs = slice((S5_T - 1 - k) * LANES, (S5_T - k) * LANES)
        bx_scr[rs, 0:nst] = pr * bbr - pi * bbi
        bx_scr[rs, nst:2 * nst] = pr * bbi + pi * bbr
        pr, pi = pr * abr - pi * abi, pr * abi + pi * abr
    a16r, a16i = pr, pi

    arec = colp_ref[0, :, 0:1]
    aimc = colp_ref[0, :, 1:2]
    dtc = jnp.exp(colp_ref[0, :, 2:3])
    erc = jnp.exp(arec * dtc)
    abrc = erc * jnp.cos(aimc * dtc)
    abic = erc * jnp.sin(aimc * dtc)
    grow_c = lax.broadcasted_iota(jnp.int32, (nst, LANES), 0) >> 6
    gcol_c = lax.broadcasted_iota(jnp.int32, (nst, LANES), 1) >> 4
    same_c = grow_c == gcol_c
    ctr = jnp.where(same_c, ctr_ref[0], 0.0)
    cti = jnp.where(same_c, cti_ref[0], 0.0)
    qr = jnp.ones_like(abrc)
    qi = jnp.zeros_like(abrc)
    for e in range(S5_T + 1):
        cs = slice(e * LANES, (e + 1) * LANES)
        cy_scr[0:nst, cs] = ctr * qr - cti * qi
        cy_scr[nst:2 * nst, cs] = -(ctr * qi + cti * qr)
        qr, qi = qr * abrc - qi * abic, qr * abic + qi * abrc

    bx_hi, bx_lo = _split_bf16(bx_scr[...])
    cy0_hi, cy0_lo = _split_bf16(cy_scr[:, 0:LANES])
    kst = (jnp.dot(bx_hi, cy0_hi, preferred_element_type=F32)
           + jnp.dot(bx_hi, cy0_lo, preferred_element_type=F32)
           + jnp.dot(bx_lo, cy0_hi, preferred_element_type=F32))
    k2_scr[:, LANES:2 * LANES] = kst.astype(BF16)
    k2_scr[0:(S5_T - 1) * LANES, 0:LANES] = kst[LANES:, :].astype(BF16)
    k2_scr[(S5_T - 1) * LANES:, 0:LANES] = jnp.zeros((LANES, LANES), BF16)

    u = u_ref[0]
    x_scr[...] = jnp.dot(u, bx_hi, preferred_element_type=F32)

    def scan_step(c, carry):
        sr, si = carry
        xr = x_scr[pl.ds(c, 1), 0:nst]
        xi = x_scr[pl.ds(c, 1), nst:2 * nst]
        x_scr[pl.ds(c, 1), 0:nst] = sr
        x_scr[pl.ds(c, 1), nst:2 * nst] = si
        return (a16r * sr - a16i * si + xr, a16r * si + a16i * sr + xi)

    zero = jnp.zeros((1, nst), F32)
    lax.fori_loop(0, S5_CHUNKS, scan_step, (zero, zero))
    s_in = x_scr[...].astype(BF16)

    cy = cy_scr[...].astype(BF16)
    d2 = jnp.concatenate([d_ref[0], d_ref[0]], axis=1)
    for t in range(0, S5_T, 2):
        y2 = jnp.dot(u[:, 0:(t + 2) * LANES], k2_scr[(S5_T - 2 - t) * LANES:, :],
                     preferred_element_type=F32)
        y2 += jnp.dot(s_in, cy[:, (t + 1) * LANES:(t + 3) * LANES],
                      preferred_element_type=F32)
        cs = slice(t * LANES, (t + 2) * LANES)
        y2 += u[:, cs].astype(F32) * d2
        y_ref[0, :, cs] = y2


def _s5_core(u2, rowp, colp, btr, bti, ctr, cti, d):
    blk3 = lambda *s: pl.BlockSpec((1,) + s, lambda b: (b, 0, 0))
    return pl.pallas_call(
        _s5_kernel,
        grid=(S5_BLOCKS,),
        in_specs=[
            blk3(S5_CHUNKS, S5_CHUNK_COLS),
            blk3(8, S5_BLOCK_STATE),
            blk3(S5_BLOCK_STATE, 3),
            blk3(SSM_GROUP, S5_BLOCK_STATE),
            blk3(SSM_GROUP, S5_BLOCK_STATE),
            blk3(S5_BLOCK_STATE, LANES),
            blk3(S5_BLOCK_STATE, LANES),
            blk3(1, LANES),
        ],
        out_specs=blk3(S5_CHUNKS, S5_CHUNK_COLS),
        out_shape=jax.ShapeDtypeStruct((S5_BLOCKS, S5_CHUNKS, S5_CHUNK_COLS), F32),
        scratch_shapes=[
            pltpu.VMEM((S5_CHUNK_COLS, 2 * S5_BLOCK_STATE), F32),
            pltpu.VMEM((2 * S5_BLOCK_STATE, (S5_T + 1) * LANES), F32),
            pltpu.VMEM((S5_CHUNK_COLS, 2 * LANES), BF16),
            pltpu.VMEM((S5_CHUNKS, 2 * S5_BLOCK_STATE), F32),
        ],
        compiler_params=pltpu.CompilerParams(
            dimension_semantics=("arbitrary",), vmem_limit_bytes=VMEM_LIMIT),
        name="s5_core",
    )(u2, rowp, colp, btr, bti, ctr, cti, d)


def _out_proj_kernel(y_ref, sgu_ref, x_ref, gluw_ref, glub_ref, g_ref, wout_ref, o_ref):
    tm = x_ref.shape[0]
    y = jnp.concatenate(
        [y_ref[b].reshape(tm, LANES) for b in range(S5_BLOCKS)], axis=1)
    y = jax.nn.gelu(y)
    gate = jax.nn.sigmoid(
        jnp.dot(y.astype(BF16), gluw_ref[...], preferred_element_type=F32) + glub_ref[...])
    a_n = _rms(y * gate, g_ref[...]).astype(BF16)
    mixed = jnp.concatenate([a_n, sgu_ref[...]], axis=1)
    o_ref[...] = x_ref[...] + jnp.dot(mixed, wout_ref[...], preferred_element_type=F32)


def _out_proj(y4, sgu_n, x, glu_w, glu_b, g, w_out):
    tm = ROW_TILE
    return pl.pallas_call(
        _out_proj_kernel,
        grid=(SEQ // tm,),
        in_specs=[
            pl.BlockSpec((S5_BLOCKS, tm // S5_T, S5_T, LANES), lambda i: (0, i, 0, 0)),
            pl.BlockSpec((tm, SGU_WIDTH), lambda i: (i, 0)),
            pl.BlockSpec((tm, D_MODEL), lambda i: (i, 0)),
            _const_spec((SSM_WIDTH, SSM_WIDTH)),
            _const_spec((1, SSM_WIDTH)),
            _const_spec((1, SSM_WIDTH)),
            _const_spec((D_MODEL, D_MODEL)),
        ],
        out_specs=pl.BlockSpec((tm, D_MODEL), lambda i: (i, 0)),
        out_shape=jax.ShapeDtypeStruct((SEQ, D_MODEL), F32),
        compiler_params=pltpu.CompilerParams(
            dimension_semantics=("arbitrary",), vmem_limit_bytes=VMEM_LIMIT),
        name="s5_glu_out_proj",
    )(y4, sgu_n, x, glu_w, glu_b, g, w_out)


def _mlp_kernel(x_ref, g_ref, wup_ref, wdn_ref, gf_ref, o_ref, h_scr):
    j = pl.program_id(1)

    @pl.when(j == 0)
    def _():
        x = x_ref[...]
        h_scr[...] = _rms(x, g_ref[...]).astype(BF16)
        o_ref[...] = x

    a = jnp.dot(h_scr[...], wup_ref[...], preferred_element_type=F32)
    r = jnp.square(jnp.maximum(a, 0.0)).astype(BF16)
    o_ref[...] += jnp.dot(r, wdn_ref[...], preferred_element_type=F32)

    @pl.when(j == pl.num_programs(1) - 1)
    def _():
        o_ref[...] = _rms(o_ref[...], gf_ref[...])


def _mlp(x1, g, w_up, w_down, gf):
    tm, tf = MLP_ROW_TILE, MLP_FF_TILE
    return pl.pallas_call(
        _mlp_kernel,
        grid=(SEQ // tm, D_FF // tf),
        in_specs=[
            pl.BlockSpec((tm, D_MODEL), lambda i, j: (i, 0)),
            _const_spec((1, D_MODEL)),
            pl.BlockSpec((D_MODEL, tf), lambda i, j: (0, j)),
            pl.BlockSpec((tf, D_MODEL), lambda i, j: (j, 0)),
            _const_spec((1, D_MODEL)),
        ],
        out_specs=pl.BlockSpec((tm, D_MODEL), lambda i, j: (i, 0)),
        out_shape=jax.ShapeDtypeStruct((SEQ, D_MODEL), F32),
        scratch_shapes=[pltpu.VMEM((tm, D_MODEL), BF16)],
        compiler_params=pltpu.CompilerParams(
            dimension_semantics=("arbitrary", "arbitrary"), vmem_limit_bytes=VMEM_LIMIT),
        name="mlp_final_norm",
    )(x1, g, w_up, w_down, gf)


def _s5_params(a_re, a_im, b_re, b_im, c_re, c_im, d, log_dt):
    nb, gpb, p, h = S5_BLOCKS, S5_GROUPS_PER_BLOCK, SSM_STATE, SSM_GROUP
    ldt = jnp.broadcast_to(log_dt[:, None], a_re.shape)
    flat = jnp.stack([a_re, a_im, ldt], axis=0).reshape(3, nb, gpb * p)
    rowp = jnp.concatenate(
        [jnp.transpose(flat, (1, 0, 2)), jnp.zeros((nb, 5, gpb * p), F32)], axis=1)
    colp = jnp.transpose(flat, (1, 2, 0))

    def b_layout(b):
        return jnp.transpose(b.reshape(nb, gpb, p, h), (0, 3, 1, 2)).reshape(nb, h, gpb * p)

    def c_layout(c):
        t = jnp.transpose(c.reshape(nb, gpb, h, p), (0, 1, 3, 2)).reshape(nb, gpb * p, h)
        return jnp.tile(t, (1, 1, gpb))

    return (rowp, colp, b_layout(b_re), b_layout(b_im), c_layout(c_re), c_layout(c_im),
            d.reshape(nb, 1, LANES))


def kernel(x, norm_mix_g, w_in, ssm_a_re, ssm_a_im, ssm_b_re, ssm_b_im, ssm_c_re, ssm_c_im,
           ssm_d, ssm_log_dt, ssm_glu_w, ssm_glu_b, sgu_ln_g, sgu_ln_b, sgu_w, sgu_b,
           out_norm_ssm_g, out_norm_sgu_g, w_out, norm_mlp_g, w_up, w_down, norm_final_g):
    assert norm_mix_g.shape[0] == 1, "single-layer problem: the MLP call applies the final norm"
    l = 0
    xs = x.reshape(SEQ, D_MODEL)
    row = lambda v: v.reshape(1, -1)
    sgu_bias = jnp.repeat(sgu_b[l].T, SGU_HEAD_DIM, axis=1)
    u4, sgu_n = _in_proj(xs, row(norm_mix_g[l]), w_in[l].astype(BF16),
                         row(sgu_ln_g[l]), row(sgu_ln_b[l]), sgu_w[l], sgu_bias,
                         row(out_norm_sgu_g[l]))
    s5p = _s5_params(ssm_a_re[l], ssm_a_im[l], ssm_b_re[l], ssm_b_im[l],
                     ssm_c_re[l], ssm_c_im[l], ssm_d[l], ssm_log_dt[l])
    y3 = _s5_core(u4.reshape(S5_BLOCKS, S5_CHUNKS, S5_CHUNK_COLS), *s5p)
    x1 = _out_proj(y3.reshape(S5_BLOCKS, S5_CHUNKS, S5_T, LANES), sgu_n, xs,
                   ssm_glu_w[l].astype(BF16), row(ssm_glu_b[l]),
                   row(out_norm_ssm_g[l]), w_out[l].astype(BF16))
    out = _mlp(x1, row(norm_mlp_g[l]), w_up[l].astype(BF16), w_down[l].astype(BF16),
               row(norm_final_g))
    return out.reshape(x.shape)
```

```python
import jax
import jax.numpy as jnp
from jax import lax
from jax.experimental import pallas as pl
from jax.experimental.pallas import tpu as pltpu

D_MODEL = 2048
SEQ = 8192
SSM_WIDTH = 1024
SSM_GROUP = 16
SSM_STATE = 64
SGU_WIDTH = 1024
SGU_HEADS = 8
SGU_HEAD_DIM = 128
SGU_CHUNK = 128
IN_WIDTH = SSM_WIDTH + 2 * SGU_WIDTH
D_FF = 4 * D_MODEL
EPS = 1e-6

LANES = 128
S5_T = 16
S5_CHUNKS = SEQ // S5_T
S5_BLOCKS = SSM_WIDTH // LANES
S5_GROUPS_PER_BLOCK = LANES // SSM_GROUP
S5_BLOCK_STATE = S5_GROUPS_PER_BLOCK * SSM_STATE
S5_CHUNK_COLS = S5_T * LANES

ROW_TILE = 512
MLP_ROW_TILE = 1024
MLP_FF_TILE = 512
VMEM_LIMIT = 60 * 1024 * 1024

F32 = jnp.float32
BF16 = jnp.bfloat16


def _rms(x, g):
    return x * lax.rsqrt(jnp.mean(x * x, axis=-1, keepdims=True) + EPS) * g


def _const_spec(shape):
    n = len(shape)
    return pl.BlockSpec(shape, lambda *_: (0,) * n, pipeline_mode=pl.Buffered(1))


def _dot(a, b):
    return jnp.dot(a, b, preferred_element_type=F32)


def _dot_nt(a, b):
    return lax.dot_general(a, b, (((1,), (1,)), ((), ())), preferred_element_type=F32)


def _in_proj_kernel(x_ref, g_ref, win_ref, lng_ref, lnb_ref, sw_ref, sbias_ref,
                    og_ref, u_ref, sgu_ref, y_scr):
    tm = x_ref.shape[0]
    h = _rms(x_ref[...], g_ref[...]).astype(BF16)
    z = _dot(h, win_ref[...])

    for b in range(S5_BLOCKS):
        u_ref[b] = z[:, b * LANES:(b + 1) * LANES]

    a = jax.nn.gelu(z[:, SSM_WIDTH:])
    u = a[:, :SGU_WIDTH]
    v = a[:, SGU_WIDTH:]
    mu = jnp.mean(v, axis=-1, keepdims=True)
    vc = v - mu
    var = jnp.mean(vc * vc, axis=-1, keepdims=True)
    vb = (vc * lax.rsqrt(var + EPS) * lng_ref[...] + lnb_ref[...]).astype(BF16)

    row = lax.broadcasted_iota(jnp.int32, (SGU_CHUNK, SGU_CHUNK), 0)
    col = lax.broadcasted_iota(jnp.int32, (SGU_CHUNK, SGU_CHUNK), 1)
    causal = row >= col
    for hd in range(SGU_HEADS):
        w = jnp.where(causal, sw_ref[hd], 0.0).astype(BF16)
        cs = slice(hd * SGU_HEAD_DIM, (hd + 1) * SGU_HEAD_DIM)
        for ck in range(tm // SGU_CHUNK):
            rs = slice(ck * SGU_CHUNK, (ck + 1) * SGU_CHUNK)
            mixed = _dot(w, vb[rs, cs])
            y_scr[rs, cs] = u[rs, cs] * (mixed + sbias_ref[:, cs])
    sgu_ref[...] = _rms(y_scr[...], og_ref[...]).astype(BF16)


def _in_proj(x, g, w_in, ln_g, ln_b, sgu_w, sgu_bias, og):
    tm = ROW_TILE
    return pl.pallas_call(
        _in_proj_kernel,
        grid=(SEQ // tm,),
        in_specs=[
            pl.BlockSpec((tm, D_MODEL), lambda i: (i, 0)),
            _const_spec((1, D_MODEL)),
            _const_spec((D_MODEL, IN_WIDTH)),
            _const_spec((1, SGU_WIDTH)),
            _const_spec((1, SGU_WIDTH)),
            _const_spec((SGU_HEADS, SGU_CHUNK, SGU_CHUNK)),
            _const_spec((SGU_CHUNK, SGU_WIDTH)),
            _const_spec((1, SGU_WIDTH)),
        ],
        out_specs=[
            pl.BlockSpec((S5_BLOCKS, tm, LANES), lambda i: (0, i, 0)),
            pl.BlockSpec((tm, SGU_WIDTH), lambda i: (i, 0)),
        ],
        out_shape=[
            jax.ShapeDtypeStruct((S5_BLOCKS, SEQ, LANES), F32),
            jax.ShapeDtypeStruct((SEQ, SGU_WIDTH), BF16),
        ],
        scratch_shapes=[pltpu.VMEM((tm, SGU_WIDTH), F32)],
        compiler_params=pltpu.CompilerParams(
            dimension_semantics=("arbitrary",), vmem_limit_bytes=VMEM_LIMIT),
        name="in_proj_sgu",
    )(x, g, w_in, ln_g, ln_b, sgu_w, sgu_bias, og)


def _s5_kernel(u_ref, rowp_ref, btr_ref, bti_ref, ctr_ref, cti_ref, d_ref,
               y_ref, ucat_scr, bx_scr, cyt_scr, k2_scr, x_scr):
    nst = S5_BLOCK_STATE

    are = rowp_ref[0, 0:1, :]
    aim = rowp_ref[0, 1:2, :]
    dt = jnp.exp(rowp_ref[0, 2:3, :])
    er = jnp.exp(are * dt)
    abr = er * jnp.cos(aim * dt)
    abi = er * jnp.sin(aim * dt)
    den = are * are + aim * aim
    cr = ((abr - 1.0) * are + abi * aim) / den
    ci = (abi * are - (abr - 1.0) * aim) / den
    btr = btr_ref[0]
    bti = bti_ref[0]
    bbr = cr * btr - ci * bti
    bbi = cr * bti + ci * btr
    ctr = ctr_ref[0]
    cti = cti_ref[0]

    grow = lax.broadcasted_iota(jnp.int32, (LANES, 2 * nst), 0) >> 4
    gcol = (lax.broadcasted_iota(jnp.int32, (LANES, 2 * nst), 1) & (nst - 1)) >> 6
    same = grow == gcol

    def expand(re, im):
        x = jnp.concatenate([re, im], axis=1)
        x = jnp.concatenate([x] * S5_GROUPS_PER_BLOCK, axis=0)
        return jnp.where(same, x, 0.0).astype(BF16)

    pr = jnp.ones_like(abr)
    pi = jnp.zeros_like(abr)
    for k in range(S5_T + 1):
        if k < S5_T:
            rs = slice((S5_T - 1 - k) * LANES, (S5_T - k) * LANES)
            bx_scr[rs, :] = expand(pr * bbr - pi * bbi, pr * bbi + pi * bbr)
        cyt_scr[k * LANES:(k + 1) * LANES, :] = expand(ctr * pr - cti * pi, -(ctr * pi + cti * pr))
        if k < S5_T:
            pr, pi = pr * abr - pi * abi, pr * abi + pi * abr
    a16r, a16i = pr, pi

    kst = _dot_nt(bx_scr[...], cyt_scr[0:LANES, :]).astype(BF16)
    k2_scr[:, LANES:2 * LANES] = kst
    k2_scr[0:(S5_T - 1) * LANES, 0:LANES] = kst[LANES:, :]
    k2_scr[(S5_T - 1) * LANES:, 0:LANES] = jnp.zeros((LANES, LANES), BF16)

    for t in range(S5_T):
        ucat_scr[:, t * LANES:(t + 1) * LANES] = (
            u_ref[0, pl.ds(t, S5_CHUNKS, stride=S5_T), :].astype(BF16))
    u = ucat_scr[...]

    x_scr[...] = _dot(u, bx_scr[...])

    def scan_step(c, carry):
        sr, si = carry
        xr = x_scr[pl.ds(c, 1), 0:nst]
        xi = x_scr[pl.ds(c, 1), nst:2 * nst]
        x_scr[pl.ds(c, 1), 0:nst] = sr
        x_scr[pl.ds(c, 1), nst:2 * nst] = si
        return (a16r * sr - a16i * si + xr, a16r * si + a16i * sr + xi)

    zero = jnp.zeros((1, nst), F32)
    lax.fori_loop(0, S5_CHUNKS, scan_step, (zero, zero))
    s_in = x_scr[...].astype(BF16)

    d = d_ref[0]
    for t in range(0, S5_T, 2):
        y2 = _dot(u[:, 0:(t + 2) * LANES], k2_scr[(S5_T - 2 - t) * LANES:, :])
        y2 += _dot_nt(s_in, cyt_scr[(t + 1) * LANES:(t + 3) * LANES, :])
        for q in range(2):
            rows = pl.ds(t + q, S5_CHUNKS, stride=S5_T)
            y_ref[0, rows, :] = y2[:, q * LANES:(q + 1) * LANES] + u_ref[0, rows, :] * d


def _s5_core(u3, rowp, btr, bti, ctr, cti, d):
    blk3 = lambda *s: pl.BlockSpec((1,) + s, lambda b: (b, 0, 0))
    return pl.pallas_call(
        _s5_kernel,
        grid=(S5_BLOCKS,),
        in_specs=[
            blk3(SEQ, LANES),
            blk3(8, S5_BLOCK_STATE),
            blk3(SSM_GROUP, S5_BLOCK_STATE),
            blk3(SSM_GROUP, S5_BLOCK_STATE),
            blk3(SSM_GROUP, S5_BLOCK_STATE),
            blk3(SSM_GROUP, S5_BLOCK_STATE),
            blk3(1, LANES),
        ],
        out_specs=blk3(SEQ, LANES),
        out_shape=jax.ShapeDtypeStruct((S5_BLOCKS, SEQ, LANES), F32),
        scratch_shapes=[
            pltpu.VMEM((S5_CHUNKS, S5_CHUNK_COLS), BF16),
            pltpu.VMEM((S5_CHUNK_COLS, 2 * S5_BLOCK_STATE), BF16),
            pltpu.VMEM(((S5_T + 1) * LANES, 2 * S5_BLOCK_STATE), BF16),
            pltpu.VMEM((S5_CHUNK_COLS, 2 * LANES), BF16),
            pltpu.VMEM((S5_CHUNKS, 2 * S5_BLOCK_STATE), F32),
        ],
        compiler_params=pltpu.CompilerParams(
            dimension_semantics=("arbitrary",), vmem_limit_bytes=VMEM_LIMIT),
        name="s5_core",
    )(u3, rowp, btr, bti, ctr, cti, d)


def _out_proj_kernel(y_ref, sgu_ref, x_ref, gluw_ref, glub_ref, g_ref, wout_ref, o_ref):
    y = jnp.concatenate([y_ref[b] for b in range(S5_BLOCKS)], axis=1)
    y = jax.nn.gelu(y)
    gate = jax.nn.sigmoid(_dot(y.astype(BF16), gluw_ref[...]) + glub_ref[...])
    a_n = _rms(y * gate, g_ref[...]).astype(BF16)
    mixed = jnp.concatenate([a_n, sgu_ref[...]], axis=1)
    o_ref[...] = x_ref[...] + _dot(mixed, wout_ref[...])


def _out_proj(y3, sgu_n, x, glu_w, glu_b, g, w_out):
    tm = ROW_TILE
    return pl.pallas_call(
        _out_proj_kernel,
        grid=(SEQ // tm,),
        in_specs=[
            pl.BlockSpec((S5_BLOCKS, tm, LANES), lambda i: (0, i, 0)),
            pl.BlockSpec((tm, SGU_WIDTH), lambda i: (i, 0)),
            pl.BlockSpec((tm, D_MODEL), lambda i: (i, 0)),
            _const_spec((SSM_WIDTH, SSM_WIDTH)),
            _const_spec((1, SSM_WIDTH)),
            _const_spec((1, SSM_WIDTH)),
            _const_spec((D_MODEL, D_MODEL)),
        ],
        out_specs=pl.BlockSpec((tm, D_MODEL), lambda i: (i, 0)),
        out_shape=jax.ShapeDtypeStruct((SEQ, D_MODEL), F32),
        compiler_params=pltpu.CompilerParams(
            dimension_semantics=("arbitrary",), vmem_limit_bytes=VMEM_LIMIT),
        name="s5_glu_out_proj",
    )(y3, sgu_n, x, glu_w, glu_b, g, w_out)


def _mlp_kernel(x_ref, g_ref, wup_ref, wdn_ref, gf_ref, o_ref, h_scr):
    j = pl.program_id(1)

    @pl.when(j == 0)
    def _():
        x = x_ref[...]
        h_scr[...] = _rms(x, g_ref[...]).astype(BF16)
        o_ref[...] = x

    a = _dot(h_scr[...], wup_ref[...].astype(BF16))
    r = jnp.square(jnp.maximum(a, 0.0)).astype(BF16)
    o_ref[...] += _dot(r, wdn_ref[...].astype(BF16))

    @pl.when(j == pl.num_programs(1) - 1)
    def _():
        o_ref[...] = _rms(o_ref[...], gf_ref[...])


def _mlp(x1, g, w_up, w_down, gf):
    tm, tf = MLP_ROW_TILE, MLP_FF_TILE
    return pl.pallas_call(
        _mlp_kernel,
        grid=(SEQ // tm, D_FF // tf),
        in_specs=[
            pl.BlockSpec((tm, D_MODEL), lambda i, j: (i, 0), pipeline_mode=pl.Buffered(1)),
            _const_spec((1, D_MODEL)),
            pl.BlockSpec((D_MODEL, tf), lambda i, j: (0, j)),
            pl.BlockSpec((tf, D_MODEL), lambda i, j: (j, 0)),
            _const_spec((1, D_MODEL)),
        ],
        out_specs=pl.BlockSpec((tm, D_MODEL), lambda i, j: (i, 0)),
        out_shape=jax.ShapeDtypeStruct((SEQ, D_MODEL), F32),
        scratch_shapes=[pltpu.VMEM((tm, D_MODEL), BF16)],
        compiler_params=pltpu.CompilerParams(
            dimension_semantics=("arbitrary", "arbitrary"), vmem_limit_bytes=VMEM_LIMIT),
        name="mlp_final_norm",
    )(x1, g, w_up, w_down, gf)


def _s5_params(a_re, a_im, b_re, b_im, c_re, c_im, d, log_dt):
    nb, gpb, p, h = S5_BLOCKS, S5_GROUPS_PER_BLOCK, SSM_STATE, SSM_GROUP
    ldt = jnp.broadcast_to(log_dt[:, None], a_re.shape)
    flat = jnp.stack([a_re, a_im, ldt], axis=0).reshape(3, nb, gpb * p)
    rowp = jnp.concatenate(
        [jnp.transpose(flat, (1, 0, 2)), jnp.zeros((nb, 5, gpb * p), F32)], axis=1)

    def b_layout(b):
        return jnp.transpose(b.reshape(nb, gpb, p, h), (0, 3, 1, 2)).reshape(nb, h, gpb * p)

    def c_layout(c):
        return jnp.transpose(c.reshape(nb, gpb, h, p), (0, 2, 1, 3)).reshape(nb, h, gpb * p)

    return (rowp, b_layout(b_re), b_layout(b_im), c_layout(c_re), c_layout(c_im),
            d.reshape(nb, 1, LANES))


def kernel(x, norm_mix_g, w_in, ssm_a_re, ssm_a_im, ssm_b_re, ssm_b_im, ssm_c_re, ssm_c_im,
           ssm_d, ssm_log_dt, ssm_glu_w, ssm_glu_b, sgu_ln_g, sgu_ln_b, sgu_w, sgu_b,
           out_norm_ssm_g, out_norm_sgu_g, w_out, norm_mlp_g, w_up, w_down, norm_final_g):
    assert norm_mix_g.shape[0] == 1, "single-layer problem: the MLP call applies the final norm"
    l = 0
    xs = x.reshape(SEQ, D_MODEL)
    row = lambda v: v.reshape(1, -1)
    sgu_bias = jnp.repeat(sgu_b[l].T, SGU_HEAD_DIM, axis=1)
    u3, sgu_n = _in_proj(xs, row(norm_mix_g[l]), w_in[l].astype(BF16),
                         row(sgu_ln_g[l]), row(sgu_ln_b[l]), sgu_w[l], sgu_bias,
                         row(out_norm_sgu_g[l]))
    s5p = _s5_params(ssm_a_re[l], ssm_a_im[l], ssm_b_re[l], ssm_b_im[l],
                     ssm_c_re[l], ssm_c_im[l], ssm_d[l], ssm_log_dt[l])
    y3 = _s5_core(u3, *s5p)
    x1 = _out_proj(y3, sgu_n, xs, ssm_glu_w[l].astype(BF16), row(ssm_glu_b[l]),
                   row(out_norm_ssm_g[l]), w_out[l].astype(BF16))
    out = _mlp(x1, row(norm_mlp_g[l]), w_up[l], w_down[l], row(norm_final_g))
    return out.reshape(x.shape)
```

```python
import jax
import jax.numpy as jnp
from jax import lax
from jax.experimental import pallas as pl
from jax.experimental.pallas import tpu as pltpu

D_MODEL = 2048
SEQ = 8192
SSM_WIDTH = 1024
SSM_GROUP = 16
SSM_STATE = 64
SGU_WIDTH = 1024
SGU_HEADS = 8
SGU_HEAD_DIM = 128
SGU_CHUNK = 128
IN_WIDTH = SSM_WIDTH + 2 * SGU_WIDTH
D_FF = 4 * D_MODEL
EPS = 1e-6

LANES = 128
S5_T = 16
S5_CHUNKS = SEQ // S5_T
S5_BLOCKS = SSM_WIDTH // LANES
S5_GROUPS_PER_BLOCK = LANES // SSM_GROUP
S5_BLOCK_STATE = S5_GROUPS_PER_BLOCK * SSM_STATE
S5_CHUNK_COLS = S5_T * LANES

ROW_TILE = 512
MLP_ROW_TILE = 1024
MLP_FF_TILE = 512
VMEM_LIMIT = 60 * 1024 * 1024

F32 = jnp.float32
BF16 = jnp.bfloat16


def _rms(x, g):
    return x * lax.rsqrt(jnp.mean(x * x, axis=-1, keepdims=True) + EPS) * g


def _const_spec(shape):
    n = len(shape)
    return pl.BlockSpec(shape, lambda *_: (0,) * n, pipeline_mode=pl.Buffered(1))


def _dot(a, b):
    return jnp.dot(a, b, preferred_element_type=F32)


def _dot_nt(a, b):
    return lax.dot_general(a, b, (((1,), (1,)), ((), ())), preferred_element_type=F32)


def _in_proj_kernel(x_ref, g_ref, win_ref, lng_ref, lnb_ref, sw_ref, sbias_ref,
                    og_ref, u_ref, sgu_ref, y_scr):
    tm = x_ref.shape[0]
    h = _rms(x_ref[...], g_ref[...]).astype(BF16)
    z = _dot(h, win_ref[...])

    for b in range(S5_BLOCKS):
        u_ref[b] = z[:, b * LANES:(b + 1) * LANES]

    a = jax.nn.gelu(z[:, SSM_WIDTH:])
    u = a[:, :SGU_WIDTH]
    v = a[:, SGU_WIDTH:]
    mu = jnp.mean(v, axis=-1, keepdims=True)
    vc = v - mu
    var = jnp.mean(vc * vc, axis=-1, keepdims=True)
    vb = (vc * lax.rsqrt(var + EPS) * lng_ref[...] + lnb_ref[...]).astype(BF16)

    row = lax.broadcasted_iota(jnp.int32, (SGU_CHUNK, SGU_CHUNK), 0)
    col = lax.broadcasted_iota(jnp.int32, (SGU_CHUNK, SGU_CHUNK), 1)
    causal = row >= col
    for hd in range(SGU_HEADS):
        w = jnp.where(causal, sw_ref[hd], 0.0).astype(BF16)
        cs = slice(hd * SGU_HEAD_DIM, (hd + 1) * SGU_HEAD_DIM)
        for ck in range(tm // SGU_CHUNK):
            rs = slice(ck * SGU_CHUNK, (ck + 1) * SGU_CHUNK)
            mixed = _dot(w, vb[rs, cs])
            y_scr[rs, cs] = u[rs, cs] * (mixed + sbias_ref[:, cs])
    sgu_ref[...] = _rms(y_scr[...], og_ref[...]).astype(BF16)


def _in_proj(x, g, w_in, ln_g, ln_b, sgu_w, sgu_bias, og):
    tm = ROW_TILE
    return pl.pallas_call(
        _in_proj_kernel,
        grid=(SEQ // tm,),
        in_specs=[
            pl.BlockSpec((tm, D_MODEL), lambda i: (i, 0)),
            _const_spec((1, D_MODEL)),
            _const_spec((D_MODEL, IN_WIDTH)),
            _const_spec((1, SGU_WIDTH)),
            _const_spec((1, SGU_WIDTH)),
            _const_spec((SGU_HEADS, SGU_CHUNK, SGU_CHUNK)),
            _const_spec((SGU_CHUNK, SGU_WIDTH)),
            _const_spec((1, SGU_WIDTH)),
        ],
        out_specs=[
            pl.BlockSpec((S5_BLOCKS, tm, LANES), lambda i: (0, i, 0)),
            pl.BlockSpec((tm, SGU_WIDTH), lambda i: (i, 0)),
        ],
        out_shape=[
            jax.ShapeDtypeStruct((S5_BLOCKS, SEQ, LANES), F32),
            jax.ShapeDtypeStruct((SEQ, SGU_WIDTH), BF16),
        ],
        scratch_shapes=[pltpu.VMEM((tm, SGU_WIDTH), F32)],
        compiler_params=pltpu.CompilerParams(
            dimension_semantics=("arbitrary",), vmem_limit_bytes=VMEM_LIMIT),
        name="in_proj_sgu",
    )(x, g, w_in, ln_g, ln_b, sgu_w, sgu_bias, og)


def _s5_kernel(u_ref, rowp_ref, btr_ref, bti_ref, ctr_ref, cti_ref, d_ref,
               y_ref, ucat_scr, bx_scr, cyt_scr, k2_scr, x_scr):
    nst = S5_BLOCK_STATE

    are = rowp_ref[0, 0:1, :]
    aim = rowp_ref[0, 1:2, :]
    dt = jnp.exp(rowp_ref[0, 2:3, :])
    er = jnp.exp(are * dt)
    abr = er * jnp.cos(aim * dt)
    abi = er * jnp.sin(aim * dt)
    den = are * are + aim * aim
    cr = ((abr - 1.0) * are + abi * aim) / den
    ci = (abi * are - (abr - 1.0) * aim) / den
    btr = btr_ref[0]
    bti = bti_ref[0]
    bbr = cr * btr - ci * bti
    bbi = cr * bti + ci * btr
    ctr = ctr_ref[0]
    cti = cti_ref[0]

    grow = lax.broadcasted_iota(jnp.int32, (LANES, 2 * nst), 0) >> 4
    gcol = (lax.broadcasted_iota(jnp.int32, (LANES, 2 * nst), 1) & (nst - 1)) >> 6
    same = grow == gcol

    def expand(re, im):
        x = jnp.concatenate([re, im], axis=1)
        x = jnp.concatenate([x] * S5_GROUPS_PER_BLOCK, axis=0)
        return jnp.where(same, x, 0.0).astype(BF16)

    pr = jnp.ones_like(abr)
    pi = jnp.zeros_like(abr)
    for k in range(S5_T + 1):
        if k < S5_T:
            rs = slice((S5_T - 1 - k) * LANES, (S5_T - k) * LANES)
            bx_scr[rs, :] = expand(pr * bbr - pi * bbi, pr * bbi + pi * bbr)
        cyt_scr[k * LANES:(k + 1) * LANES, :] = expand(ctr * pr - cti * pi, -(ctr * pi + cti * pr))
        if k < S5_T:
            pr, pi = pr * abr - pi * abi, pr * abi + pi * abr
    a16r, a16i = pr, pi

    kst_f32 = _dot_nt(bx_scr[...], cyt_scr[0:LANES, :])
    kst = kst_f32.astype(BF16)
    eye = (lax.broadcasted_iota(jnp.int32, (LANES, LANES), 0)
           == lax.broadcasted_iota(jnp.int32, (LANES, LANES), 1))
    k0 = (kst_f32[(S5_T - 1) * LANES:, :] + jnp.where(eye, d_ref[0], 0.0)).astype(BF16)
    k2_scr[:, LANES:2 * LANES] = kst
    k2_scr[(S5_T - 1) * LANES:, LANES:2 * LANES] = k0
    k2_scr[0:(S5_T - 1) * LANES, 0:LANES] = kst[LANES:, :]
    k2_scr[(S5_T - 2) * LANES:(S5_T - 1) * LANES, 0:LANES] = k0
    k2_scr[(S5_T - 1) * LANES:, 0:LANES] = jnp.zeros((LANES, LANES), BF16)

    for t in range(S5_T):
        ucat_scr[:, t * LANES:(t + 1) * LANES] = (
            u_ref[0, pl.ds(t, S5_CHUNKS, stride=S5_T), :].astype(BF16))
    u = ucat_scr[...]

    x_scr[...] = _dot(u, bx_scr[...])

    def scan_step(c, carry):
        sr, si = carry
        xr = x_scr[pl.ds(c, 1), 0:nst]
        xi = x_scr[pl.ds(c, 1), nst:2 * nst]
        x_scr[pl.ds(c, 1), 0:nst] = sr
        x_scr[pl.ds(c, 1), nst:2 * nst] = si
        return (a16r * sr - a16i * si + xr, a16r * si + a16i * sr + xi)

    zero = jnp.zeros((1, nst), F32)
    lax.fori_loop(0, S5_CHUNKS, scan_step, (zero, zero))
    s_in = x_scr[...].astype(BF16)

    for t in range(0, S5_T, 2):
        y2 = _dot(u[:, 0:(t + 2) * LANES], k2_scr[(S5_T - 2 - t) * LANES:, :])
        y2 += _dot_nt(s_in, cyt_scr[(t + 1) * LANES:(t + 3) * LANES, :])
        for q in range(2):
            y_ref[0, pl.ds(t + q, S5_CHUNKS, stride=S5_T), :] = y2[:, q * LANES:(q + 1) * LANES]


def _s5_core(u3, rowp, btr, bti, ctr, cti, d):
    blk3 = lambda *s: pl.BlockSpec((1,) + s, lambda b: (b, 0, 0))
    return pl.pallas_call(
        _s5_kernel,
        grid=(S5_BLOCKS,),
        in_specs=[
            blk3(SEQ, LANES),
            blk3(8, S5_BLOCK_STATE),
            blk3(SSM_GROUP, S5_BLOCK_STATE),
            blk3(SSM_GROUP, S5_BLOCK_STATE),
            blk3(SSM_GROUP, S5_BLOCK_STATE),
            blk3(SSM_GROUP, S5_BLOCK_STATE),
            blk3(1, LANES),
        ],
        out_specs=blk3(SEQ, LANES),
        out_shape=jax.ShapeDtypeStruct((S5_BLOCKS, SEQ, LANES), F32),
        scratch_shapes=[
            pltpu.VMEM((S5_CHUNKS, S5_CHUNK_COLS), BF16),
            pltpu.VMEM((S5_CHUNK_COLS, 2 * S5_BLOCK_STATE), BF16),
            pltpu.VMEM(((S5_T + 1) * LANES, 2 * S5_BLOCK_STATE), BF16),
            pltpu.VMEM((S5_CHUNK_COLS, 2 * LANES), BF16),
            pltpu.VMEM((S5_CHUNKS, 2 * S5_BLOCK_STATE), F32),
        ],
        compiler_params=pltpu.CompilerParams(
            dimension_semantics=("arbitrary",), vmem_limit_bytes=VMEM_LIMIT),
        name="s5_core",
    )(u3, rowp, btr, bti, ctr, cti, d)


def _out_proj_kernel(y_ref, sgu_ref, x_ref, gluw_ref, glub_ref, g_ref, wout_ref, o_ref):
    y = jnp.concatenate([y_ref[b] for b in range(S5_BLOCKS)], axis=1)
    y = jax.nn.gelu(y)
    gate = jax.nn.sigmoid(_dot(y.astype(BF16), gluw_ref[...]) + glub_ref[...])
    a_n = _rms(y * gate, g_ref[...]).astype(BF16)
    mixed = jnp.concatenate([a_n, sgu_ref[...]], axis=1)
    o_ref[...] = x_ref[...] + _dot(mixed, wout_ref[...])


def _out_proj(y3, sgu_n, x, glu_w, glu_b, g, w_out):
    tm = ROW_TILE
    return pl.pallas_call(
        _out_proj_kernel,
        grid=(SEQ // tm,),
        in_specs=[
            pl.BlockSpec((S5_BLOCKS, tm, LANES), lambda i: (0, i, 0)),
            pl.BlockSpec((tm, SGU_WIDTH), lambda i: (i, 0)),
            pl.BlockSpec((tm, D_MODEL), lambda i: (i, 0)),
            _const_spec((SSM_WIDTH, SSM_WIDTH)),
            _const_spec((1, SSM_WIDTH)),
            _const_spec((1, SSM_WIDTH)),
            _const_spec((D_MODEL, D_MODEL)),
        ],
        out_specs=pl.BlockSpec((tm, D_MODEL), lambda i: (i, 0)),
        out_shape=jax.ShapeDtypeStruct((SEQ, D_MODEL), F32),
        compiler_params=pltpu.CompilerParams(
            dimension_semantics=("arbitrary",), vmem_limit_bytes=VMEM_LIMIT),
        name="s5_glu_out_proj",
    )(y3, sgu_n, x, glu_w, glu_b, g, w_out)


def _mlp_step(x_ref, g_ref, wu, wd, gf_ref, o_ref, h_scr):
    j = pl.program_id(1)

    @pl.when(j == 0)
    def _():
        x = x_ref[...]
        h_scr[...] = _rms(x, g_ref[...]).astype(BF16)
        o_ref[...] = x

    r = jnp.square(jnp.maximum(_dot(h_scr[...], wu), 0.0)).astype(BF16)
    o_ref[...] += _dot(r, wd)

    @pl.when(j == pl.num_programs(1) - 1)
    def _():
        o_ref[...] = _rms(o_ref[...], gf_ref[...])


def _mlp_first_kernel(x_ref, g_ref, wup_ref, wdn_ref, gf_ref, o_ref, wupb_ref, wdnb_ref, h_scr):
    wu = wup_ref[...].astype(BF16)
    wd = wdn_ref[...].astype(BF16)
    wupb_ref[0] = wu
    wdnb_ref[...] = wd
    _mlp_step(x_ref, g_ref, wu, wd, gf_ref, o_ref, h_scr)


def _mlp_rest_kernel(x_ref, g_ref, wupb_ref, wdnb_ref, gf_ref, first_ref, o_ref, h_scr):
    del first_ref
    _mlp_step(x_ref, g_ref, wupb_ref[0], wdnb_ref[...], gf_ref, o_ref, h_scr)


def _mlp(x1, g, w_up, w_down, gf):
    tm, tf = MLP_ROW_TILE, MLP_FF_TILE
    nj = D_FF // tf
    params = pltpu.CompilerParams(
        dimension_semantics=("arbitrary", "arbitrary"), vmem_limit_bytes=VMEM_LIMIT)
    out_shape = jax.ShapeDtypeStruct((SEQ, D_MODEL), F32)
    scratch = [pltpu.VMEM((tm, D_MODEL), BF16)]
    wupb_spec = pl.BlockSpec((1, D_MODEL, tf), lambda i, j: (j, 0, 0))
    wdnb_spec = pl.BlockSpec((tf, D_MODEL), lambda i, j: (j, 0))

    first, wupb, wdnb = pl.pallas_call(
        _mlp_first_kernel,
        grid=(1, nj),
        in_specs=[
            pl.BlockSpec((tm, D_MODEL), lambda i, j: (0, 0), pipeline_mode=pl.Buffered(1)),
            _const_spec((1, D_MODEL)),
            pl.BlockSpec((D_MODEL, tf), lambda i, j: (0, j)),
            pl.BlockSpec((tf, D_MODEL), lambda i, j: (j, 0)),
            _const_spec((1, D_MODEL)),
        ],
        out_specs=[pl.BlockSpec((tm, D_MODEL), lambda i, j: (0, 0)), wupb_spec, wdnb_spec],
        out_shape=[out_shape,
                   jax.ShapeDtypeStruct((nj, D_MODEL, tf), BF16),
                   jax.ShapeDtypeStruct((D_FF, D_MODEL), BF16)],
        scratch_shapes=scratch,
        compiler_params=params,
        name="mlp_first_tile",
    )(x1, g, w_up, w_down, gf)

    return pl.pallas_call(
        _mlp_rest_kernel,
        grid=(SEQ // tm - 1, nj),
        in_specs=[
            pl.BlockSpec((tm, D_MODEL), lambda i, j: (i + 1, 0), pipeline_mode=pl.Buffered(1)),
            _const_spec((1, D_MODEL)),
            wupb_spec,
            wdnb_spec,
            _const_spec((1, D_MODEL)),
            pl.BlockSpec(memory_space=pl.ANY),
        ],
        out_specs=pl.BlockSpec((tm, D_MODEL), lambda i, j: (i + 1, 0)),
        out_shape=out_shape,
        input_output_aliases={5: 0},
        scratch_shapes=scratch,
        compiler_params=params,
        name="mlp_rest_tiles",
    )(x1, g, wupb, wdnb, gf, first)


def _s5_params(a_re, a_im, b_re, b_im, c_re, c_im, d, log_dt):
    nb, gpb, p, h = S5_BLOCKS, S5_GROUPS_PER_BLOCK, SSM_STATE, SSM_GROUP
    ldt = jnp.broadcast_to(log_dt[:, None], a_re.shape)
    flat = jnp.stack([a_re, a_im, ldt], axis=0).reshape(3, nb, gpb * p)
    rowp = jnp.concatenate(
        [jnp.transpose(flat, (1, 0, 2)), jnp.zeros((nb, 5, gpb * p), F32)], axis=1)

    def b_layout(b):
        return jnp.transpose(b.reshape(nb, gpb, p, h), (0, 3, 1, 2)).reshape(nb, h, gpb * p)

    def c_layout(c):
        return jnp.transpose(c.reshape(nb, gpb, h, p), (0, 2, 1, 3)).reshape(nb, h, gpb * p)

    return (rowp, b_layout(b_re), b_layout(b_im), c_layout(c_re), c_layout(c_im),
            d.reshape(nb, 1, LANES))


def kernel(x, norm_mix_g, w_in, ssm_a_re, ssm_a_im, ssm_b_re, ssm_b_im, ssm_c_re, ssm_c_im,
           ssm_d, ssm_log_dt, ssm_glu_w, ssm_glu_b, sgu_ln_g, sgu_ln_b, sgu_w, sgu_b,
           out_norm_ssm_g, out_norm_sgu_g, w_out, norm_mlp_g, w_up, w_down, norm_final_g):
    assert norm_mix_g.shape[0] == 1, "single-layer problem: the MLP call applies the final norm"
    l = 0
    xs = x.reshape(SEQ, D_MODEL)
    row = lambda v: v.reshape(1, -1)
    sgu_bias = jnp.repeat(sgu_b[l].T, SGU_HEAD_DIM, axis=1)
    u3, sgu_n = _in_proj(xs, row(norm_mix_g[l]), w_in[l].astype(BF16),
                         row(sgu_ln_g[l]), row(sgu_ln_b[l]), sgu_w[l], sgu_bias,
                         row(out_norm_sgu_g[l]))
    s5p = _s5_params(ssm_a_re[l], ssm_a_im[l], ssm_b_re[l], ssm_b_im[l],
                     ssm_c_re[l], ssm_c_im[l], ssm_d[l], ssm_log_dt[l])
    y3 = _s5_core(u3, *s5p)
    x1 = _out_proj(y3, sgu_n, xs, ssm_glu_w[l].astype(BF16), row(ssm_glu_b[l]),
                   row(out_norm_ssm_g[l]), w_out[l].astype(BF16))
    out = _mlp(x1, row(norm_mlp_g[l]), w_up[l], w_down[l], row(norm_final_g))
    return out.reshape(x.shape)
```

```python
import jax
import jax.numpy as jnp
from jax import lax
from jax.experimental import pallas as pl
from jax.experimental.pallas import tpu as pltpu

D_MODEL = 2048
SEQ = 8192
SSM_WIDTH = 1024
SSM_GROUP = 16
SSM_STATE = 64
SGU_WIDTH = 1024
SGU_HEADS = 8
SGU_HEAD_DIM = 128
SGU_CHUNK = 128
IN_WIDTH = SSM_WIDTH + 2 * SGU_WIDTH
D_FF = 4 * D_MODEL
EPS = 1e-6

LANES = 128
S5_T = 16
S5_CHUNKS = SEQ // S5_T
S5_BLOCKS = SSM_WIDTH // LANES
S5_GROUPS_PER_BLOCK = LANES // SSM_GROUP
S5_BLOCK_STATE = S5_GROUPS_PER_BLOCK * SSM_STATE
S5_CHUNK_COLS = S5_T * LANES

ROW_TILE = 512
MLP_ROW_TILE = 1024
MLP_FF_TILE = 512
VMEM_LIMIT = 60 * 1024 * 1024

F32 = jnp.float32
BF16 = jnp.bfloat16


def _rms(x, g):
    return x * lax.rsqrt(jnp.mean(x * x, axis=-1, keepdims=True) + EPS) * g


def _const_spec(shape):
    n = len(shape)
    return pl.BlockSpec(shape, lambda *_: (0,) * n, pipeline_mode=pl.Buffered(1))


def _dot(a, b):
    return jnp.dot(a, b, preferred_element_type=F32)


def _dot_nt(a, b):
    return lax.dot_general(a, b, (((1,), (1,)), ((), ())), preferred_element_type=F32)


def _in_proj_kernel(x_ref, g_ref, win_ref, lng_ref, lnb_ref, sw_ref, sbias_ref,
                    og_ref, u_ref, sgu_ref, y_scr):
    tm = x_ref.shape[0]
    h = _rms(x_ref[...], g_ref[...]).astype(BF16)
    z = _dot(h, win_ref[...])

    for b in range(S5_BLOCKS):
        u_ref[b] = z[:, b * LANES:(b + 1) * LANES]

    a = jax.nn.gelu(z[:, SSM_WIDTH:])
    u = a[:, :SGU_WIDTH]
    v = a[:, SGU_WIDTH:]
    mu = jnp.mean(v, axis=-1, keepdims=True)
    vc = v - mu
    var = jnp.mean(vc * vc, axis=-1, keepdims=True)
    vb = (vc * lax.rsqrt(var + EPS) * lng_ref[...] + lnb_ref[...]).astype(BF16)

    row = lax.broadcasted_iota(jnp.int32, (SGU_CHUNK, SGU_CHUNK), 0)
    col = lax.broadcasted_iota(jnp.int32, (SGU_CHUNK, SGU_CHUNK), 1)
    causal = row >= col
    for hd in range(SGU_HEADS):
        w = jnp.where(causal, sw_ref[hd], 0.0).astype(BF16)
        cs = slice(hd * SGU_HEAD_DIM, (hd + 1) * SGU_HEAD_DIM)
        for ck in range(tm // SGU_CHUNK):
            rs = slice(ck * SGU_CHUNK, (ck + 1) * SGU_CHUNK)
            mixed = _dot(w, vb[rs, cs])
            y_scr[rs, cs] = u[rs, cs] * (mixed + sbias_ref[:, cs])
    sgu_ref[...] = _rms(y_scr[...], og_ref[...]).astype(BF16)


def _in_proj(x, g, w_in, ln_g, ln_b, sgu_w, sgu_bias, og):
    tm = ROW_TILE
    return pl.pallas_call(
        _in_proj_kernel,
        grid=(SEQ // tm,),
        in_specs=[
            pl.BlockSpec((tm, D_MODEL), lambda i: (i, 0)),
            _const_spec((1, D_MODEL)),
            _const_spec((D_MODEL, IN_WIDTH)),
            _const_spec((1, SGU_WIDTH)),
            _const_spec((1, SGU_WIDTH)),
            _const_spec((SGU_HEADS, SGU_CHUNK, SGU_CHUNK)),
            _const_spec((SGU_CHUNK, SGU_WIDTH)),
            _const_spec((1, SGU_WIDTH)),
        ],
        out_specs=[
            pl.BlockSpec((S5_BLOCKS, tm, LANES), lambda i: (0, i, 0)),
            pl.BlockSpec((tm, SGU_WIDTH), lambda i: (i, 0)),
        ],
        out_shape=[
            jax.ShapeDtypeStruct((S5_BLOCKS, SEQ, LANES), F32),
            jax.ShapeDtypeStruct((SEQ, SGU_WIDTH), BF16),
        ],
        scratch_shapes=[pltpu.VMEM((tm, SGU_WIDTH), F32)],
        compiler_params=pltpu.CompilerParams(
            dimension_semantics=("arbitrary",), vmem_limit_bytes=VMEM_LIMIT),
        name="in_proj_sgu",
    )(x, g, w_in, ln_g, ln_b, sgu_w, sgu_bias, og)


def _s5_kernel(u_ref, rowp_ref, btr_ref, bti_ref, ctr_ref, cti_ref, d_ref,
               y_ref, ucat_scr, bx_scr, cyt_scr, k2_scr, x_scr):
    nst = S5_BLOCK_STATE

    are = rowp_ref[0, 0:1, :]
    aim = rowp_ref[0, 1:2, :]
    dt = jnp.exp(rowp_ref[0, 2:3, :])
    er = jnp.exp(are * dt)
    abr = er * jnp.cos(aim * dt)
    abi = er * jnp.sin(aim * dt)
    den = are * are + aim * aim
    cr = ((abr - 1.0) * are + abi * aim) / den
    ci = (abi * are - (abr - 1.0) * aim) / den
    btr = btr_ref[0]
    bti = bti_ref[0]
    bbr = cr * btr - ci * bti
    bbi = cr * bti + ci * btr
    ctr = ctr_ref[0]
    cti = cti_ref[0]

    grow = lax.broadcasted_iota(jnp.int32, (LANES, 2 * nst), 0) >> 4
    gcol = (lax.broadcasted_iota(jnp.int32, (LANES, 2 * nst), 1) & (nst - 1)) >> 6
    same = grow == gcol

    def expand(re, im):
        x = jnp.concatenate([re, im], axis=1)
        x = jnp.concatenate([x] * S5_GROUPS_PER_BLOCK, axis=0)
        return jnp.where(same, x, 0.0).astype(BF16)

    pr = jnp.ones_like(abr)
    pi = jnp.zeros_like(abr)
    for k in range(S5_T + 1):
        if k < S5_T:
            rs = slice((S5_T - 1 - k) * LANES, (S5_T - k) * LANES)
            bx_scr[rs, :] = expand(pr * bbr - pi * bbi, pr * bbi + pi * bbr)
        cyt_scr[k * LANES:(k + 1) * LANES, :] = expand(ctr * pr - cti * pi, -(ctr * pi + cti * pr))
        if k < S5_T:
            pr, pi = pr * abr - pi * abi, pr * abi + pi * abr
    a16r, a16i = pr, pi

    kst_f32 = _dot_nt(bx_scr[...], cyt_scr[0:LANES, :])
    kst = kst_f32.astype(BF16)
    eye = (lax.broadcasted_iota(jnp.int32, (LANES, LANES), 0)
           == lax.broadcasted_iota(jnp.int32, (LANES, LANES), 1))
    k0 = (kst_f32[(S5_T - 1) * LANES:, :] + jnp.where(eye, d_ref[0], 0.0)).astype(BF16)
    k2_scr[:, LANES:2 * LANES] = kst
    k2_scr[(S5_T - 1) * LANES:, LANES:2 * LANES] = k0
    k2_scr[0:(S5_T - 1) * LANES, 0:LANES] = kst[LANES:, :]
    k2_scr[(S5_T - 2) * LANES:(S5_T - 1) * LANES, 0:LANES] = k0
    k2_scr[(S5_T - 1) * LANES:, 0:LANES] = jnp.zeros((LANES, LANES), BF16)

    for t in range(S5_T):
        ucat_scr[:, t * LANES:(t + 1) * LANES] = (
            u_ref[0, pl.ds(t, S5_CHUNKS, stride=S5_T), :].astype(BF16))
    u = ucat_scr[...]

    x_scr[...] = _dot(u, bx_scr[...])

    def scan_step(c, carry):
        sr, si = carry
        xr = x_scr[pl.ds(c, 1), 0:nst]
        xi = x_scr[pl.ds(c, 1), nst:2 * nst]
        x_scr[pl.ds(c, 1), 0:nst] = sr
        x_scr[pl.ds(c, 1), nst:2 * nst] = si
        return (a16r * sr - a16i * si + xr, a16r * si + a16i * sr + xi)

    zero = jnp.zeros((1, nst), F32)
    lax.fori_loop(0, S5_CHUNKS, scan_step, (zero, zero))
    s_in = x_scr[...].astype(BF16)

    for t in range(0, S5_T, 2):
        y2 = _dot(u[:, 0:(t + 2) * LANES], k2_scr[(S5_T - 2 - t) * LANES:, :])
        y2 += _dot_nt(s_in, cyt_scr[(t + 1) * LANES:(t + 3) * LANES, :])
        for q in range(2):
            y_ref[0, pl.ds(t + q, S5_CHUNKS, stride=S5_T), :] = y2[:, q * LANES:(q + 1) * LANES]


def _s5_core(u3, rowp, btr, bti, ctr, cti, d):
    blk3 = lambda *s: pl.BlockSpec((1,) + s, lambda b: (b, 0, 0))
    return pl.pallas_call(
        _s5_kernel,
        grid=(S5_BLOCKS,),
        in_specs=[
            blk3(SEQ, LANES),
            blk3(8, S5_BLOCK_STATE),
            blk3(SSM_GROUP, S5_BLOCK_STATE),
            blk3(SSM_GROUP, S5_BLOCK_STATE),
            blk3(SSM_GROUP, S5_BLOCK_STATE),
            blk3(SSM_GROUP, S5_BLOCK_STATE),
            blk3(1, LANES),
        ],
        out_specs=blk3(SEQ, LANES),
        out_shape=jax.ShapeDtypeStruct((S5_BLOCKS, SEQ, LANES), F32),
        scratch_shapes=[
            pltpu.VMEM((S5_CHUNKS, S5_CHUNK_COLS), BF16),
            pltpu.VMEM((S5_CHUNK_COLS, 2 * S5_BLOCK_STATE), BF16),
            pltpu.VMEM(((S5_T + 1) * LANES, 2 * S5_BLOCK_STATE), BF16),
            pltpu.VMEM((S5_CHUNK_COLS, 2 * LANES), BF16),
            pltpu.VMEM((S5_CHUNKS, 2 * S5_BLOCK_STATE), F32),
        ],
        compiler_params=pltpu.CompilerParams(
            dimension_semantics=("arbitrary",), vmem_limit_bytes=VMEM_LIMIT),
        name="s5_core",
    )(u3, rowp, btr, bti, ctr, cti, d)


def _out_proj_kernel(y_ref, sgu_ref, x_ref, gluw_ref, glub_ref, g_ref, wout_ref, o_ref):
    y = jnp.concatenate([y_ref[b] for b in range(S5_BLOCKS)], axis=1)
    y = jax.nn.gelu(y)
    gate = jax.nn.sigmoid(_dot(y.astype(BF16), gluw_ref[...]) + glub_ref[...])
    a_n = _rms(y * gate, g_ref[...]).astype(BF16)
    mixed = jnp.concatenate([a_n, sgu_ref[...]], axis=1)
    o_ref[...] = x_ref[...] + _dot(mixed, wout_ref[...])


def _out_proj(y3, sgu_n, x, glu_w, glu_b, g, w_out):
    tm = ROW_TILE
    return pl.pallas_call(
        _out_proj_kernel,
        grid=(SEQ // tm,),
        in_specs=[
            pl.BlockSpec((S5_BLOCKS, tm, LANES), lambda i: (0, i, 0)),
            pl.BlockSpec((tm, SGU_WIDTH), lambda i: (i, 0)),
            pl.BlockSpec((tm, D_MODEL), lambda i: (i, 0)),
            _const_spec((SSM_WIDTH, SSM_WIDTH)),
            _const_spec((1, SSM_WIDTH)),
            _const_spec((1, SSM_WIDTH)),
            _const_spec((D_MODEL, D_MODEL)),
        ],
        out_specs=pl.BlockSpec((tm, D_MODEL), lambda i: (i, 0)),
        out_shape=jax.ShapeDtypeStruct((SEQ, D_MODEL), F32),
        compiler_params=pltpu.CompilerParams(
            dimension_semantics=("arbitrary",), vmem_limit_bytes=VMEM_LIMIT),
        name="s5_glu_out_proj",
    )(y3, sgu_n, x, glu_w, glu_b, g, w_out)


def _mlp_step(x_ref, g_ref, wu, wd, gf_ref, o_ref, h_scr):
    j = pl.program_id(1)

    @pl.when(j == 0)
    def _():
        x = x_ref[...]
        h_scr[...] = _rms(x, g_ref[...]).astype(BF16)
        o_ref[...] = x

    r = jnp.square(jnp.maximum(_dot(h_scr[...], wu), 0.0)).astype(BF16)
    o_ref[...] += _dot(r, wd)

    @pl.when(j == pl.num_programs(1) - 1)
    def _():
        o_ref[...] = _rms(o_ref[...], gf_ref[...])


def _mlp_first_kernel(x_ref, g_ref, wup_ref, wdn_ref, gf_ref, o_ref, wupb_ref, wdnb_ref, h_scr):
    wu = wup_ref[...].astype(BF16)
    wd = wdn_ref[...].astype(BF16)
    wupb_ref[...] = wu
    wdnb_ref[...] = wd
    _mlp_step(x_ref, g_ref, wu, wd, gf_ref, o_ref, h_scr)


def _mlp_rest_kernel(x_ref, g_ref, wupb_ref, wdnb_ref, gf_ref, first_ref, o_ref, h_scr):
    del first_ref
    _mlp_step(x_ref, g_ref, wupb_ref[...], wdnb_ref[...], gf_ref, o_ref, h_scr)


def _mlp(x1, g, w_up, w_down, gf):
    tm, tf = MLP_ROW_TILE, MLP_FF_TILE
    nj = D_FF // tf
    params = pltpu.CompilerParams(
        dimension_semantics=("arbitrary", "arbitrary"), vmem_limit_bytes=VMEM_LIMIT)
    out_shape = jax.ShapeDtypeStruct((SEQ, D_MODEL), F32)
    scratch = [pltpu.VMEM((tm, D_MODEL), BF16)]
    wupb_spec = pl.BlockSpec((D_MODEL, tf), lambda i, j: (j, 0))
    wdnb_spec = pl.BlockSpec((tf, D_MODEL), lambda i, j: (j, 0))

    first, wupb, wdnb = pl.pallas_call(
        _mlp_first_kernel,
        grid=(1, nj),
        in_specs=[
            pl.BlockSpec((tm, D_MODEL), lambda i, j: (0, 0), pipeline_mode=pl.Buffered(1)),
            _const_spec((1, D_MODEL)),
            pl.BlockSpec((D_MODEL, tf), lambda i, j: (0, j)),
            pl.BlockSpec((tf, D_MODEL), lambda i, j: (j, 0)),
            _const_spec((1, D_MODEL)),
        ],
        out_specs=[pl.BlockSpec((tm, D_MODEL), lambda i, j: (0, 0)), wupb_spec, wdnb_spec],
        out_shape=[out_shape,
                   jax.ShapeDtypeStruct((nj * D_MODEL, tf), BF16),
                   jax.ShapeDtypeStruct((D_FF, D_MODEL), BF16)],
        scratch_shapes=scratch,
        compiler_params=params,
        name="mlp_first_tile",
    )(x1, g, w_up, w_down, gf)

    return pl.pallas_call(
        _mlp_rest_kernel,
        grid=(SEQ // tm - 1, nj),
        in_specs=[
            pl.BlockSpec((tm, D_MODEL), lambda i, j: (i + 1, 0)),
            _const_spec((1, D_MODEL)),
            wupb_spec,
            wdnb_spec,
            _const_spec((1, D_MODEL)),
            pl.BlockSpec(memory_space=pl.ANY),
        ],
        out_specs=pl.BlockSpec((tm, D_MODEL), lambda i, j: (i + 1, 0)),
        out_shape=out_shape,
        input_output_aliases={5: 0},
        scratch_shapes=scratch,
        compiler_params=params,
        name="mlp_rest_tiles",
    )(x1, g, wupb, wdnb, gf, first)


def _s5_params(a_re, a_im, b_re, b_im, c_re, c_im, d, log_dt):
    nb, gpb, p, h = S5_BLOCKS, S5_GROUPS_PER_BLOCK, SSM_STATE, SSM_GROUP
    ldt = jnp.broadcast_to(log_dt[:, None], a_re.shape)
    flat = jnp.stack([a_re, a_im, ldt], axis=0).reshape(3, nb, gpb * p)
    rowp = jnp.concatenate(
        [jnp.transpose(flat, (1, 0, 2)), jnp.zeros((nb, 5, gpb * p), F32)], axis=1)

    def b_layout(b):
        return jnp.transpose(b.reshape(nb, gpb, p, h), (0, 3, 1, 2)).reshape(nb, h, gpb * p)

    def c_layout(c):
        return jnp.transpose(c.reshape(nb, gpb, h, p), (0, 2, 1, 3)).reshape(nb, h, gpb * p)

    return (rowp, b_layout(b_re), b_layout(b_im), c_layout(c_re), c_layout(c_im),
            d.reshape(nb, 1, LANES))


def kernel(x, norm_mix_g, w_in, ssm_a_re, ssm_a_im, ssm_b_re, ssm_b_im, ssm_c_re, ssm_c_im,
           ssm_d, ssm_log_dt, ssm_glu_w, ssm_glu_b, sgu_ln_g, sgu_ln_b, sgu_w, sgu_b,
           out_norm_ssm_g, out_norm_sgu_g, w_out, norm_mlp_g, w_up, w_down, norm_final_g):
    assert norm_mix_g.shape[0] == 1, "single-layer problem: the MLP call applies the final norm"
    l = 0
    xs = x.reshape(SEQ, D_MODEL)
    row = lambda v: v.reshape(1, -1)
    sgu_bias = jnp.repeat(sgu_b[l].T, SGU_HEAD_DIM, axis=1)
    u3, sgu_n = _in_proj(xs, row(norm_mix_g[l]), w_in[l].astype(BF16),
                         row(sgu_ln_g[l]), row(sgu_ln_b[l]), sgu_w[l], sgu_bias,
                         row(out_norm_sgu_g[l]))
    s5p = _s5_params(ssm_a_re[l], ssm_a_im[l], ssm_b_re[l], ssm_b_im[l],
                     ssm_c_re[l], ssm_c_im[l], ssm_d[l], ssm_log_dt[l])
    y3 = _s5_core(u3, *s5p)
    x1 = _out_proj(y3, sgu_n, xs, ssm_glu_w[l].astype(BF16), row(ssm_glu_b[l]),
                   row(out_norm_ssm_g[l]), w_out[l].astype(BF16))
    out = _mlp(x1, row(norm_mlp_g[l]), w_up[l], w_down[l], row(norm_final_g))
    return out.reshape(x.shape)
```

```python
import jax
import jax.numpy as jnp
from jax import lax
from jax.experimental import pallas as pl
from jax.experimental.pallas import tpu as pltpu

D_MODEL = 2048
SEQ = 8192
SSM_WIDTH = 1024
SSM_GROUP = 16
SSM_STATE = 64
SGU_WIDTH = 1024
SGU_HEADS = 8
SGU_HEAD_DIM = 128
SGU_CHUNK = 128
IN_WIDTH = SSM_WIDTH + 2 * SGU_WIDTH
D_FF = 4 * D_MODEL
EPS = 1e-6

LANES = 128
S5_T = 16
S5_CHUNKS = SEQ // S5_T
S5_BLOCKS = SSM_WIDTH // LANES
S5_GROUPS_PER_BLOCK = LANES // SSM_GROUP
S5_BLOCK_STATE = S5_GROUPS_PER_BLOCK * SSM_STATE
S5_CHUNK_COLS = S5_T * LANES

ROW_TILE = 512
SUB_ROWS = 256
PIPE_DEPTH = 2
MLP_ROW_TILE = 1024
MLP_FF_TILE = 512
VMEM_LIMIT = 60 * 1024 * 1024

F32 = jnp.float32
BF16 = jnp.bfloat16


def _rms(x, g):
    return x * lax.rsqrt(jnp.mean(x * x, axis=-1, keepdims=True) + EPS) * g


def _const_spec(shape):
    n = len(shape)
    return pl.BlockSpec(shape, lambda *_: (0,) * n, pipeline_mode=pl.Buffered(1))


def _dot(a, b):
    return jnp.dot(a, b, preferred_element_type=F32)


def _dot_nt(a, b):
    return lax.dot_general(a, b, (((1,), (1,)), ((), ())), preferred_element_type=F32)


def _in_proj_kernel(x_ref, g_ref, win_ref, lng_ref, lnb_ref, sw_ref, sbias_ref,
                    og_ref, u_ref, sgu_ref, y_scr):
    tm = x_ref.shape[0]
    row = lax.broadcasted_iota(jnp.int32, (SGU_CHUNK, SGU_CHUNK), 0)
    col = lax.broadcasted_iota(jnp.int32, (SGU_CHUNK, SGU_CHUNK), 1)
    causal = row >= col
    ws = [jnp.where(causal, sw_ref[hd], 0.0).astype(BF16) for hd in range(SGU_HEADS)]

    def project(r0):
        h = _rms(x_ref[r0:r0 + SUB_ROWS, :], g_ref[...]).astype(BF16)
        return _dot(h, win_ref[...])

    def mix(r0, z):
        for b in range(S5_BLOCKS):
            u_ref[b, r0:r0 + SUB_ROWS, :] = z[:, b * LANES:(b + 1) * LANES]

        a = jax.nn.gelu(z[:, SSM_WIDTH:])
        u = a[:, :SGU_WIDTH]
        v = a[:, SGU_WIDTH:]
        mu = jnp.mean(v, axis=-1, keepdims=True)
        vc = v - mu
        var = jnp.mean(vc * vc, axis=-1, keepdims=True)
        vb = (vc * lax.rsqrt(var + EPS) * lng_ref[...] + lnb_ref[...]).astype(BF16)
        for hd in range(SGU_HEADS):
            cs = slice(hd * SGU_HEAD_DIM, (hd + 1) * SGU_HEAD_DIM)
            for ck in range(SUB_ROWS // SGU_CHUNK):
                rs = slice(ck * SGU_CHUNK, (ck + 1) * SGU_CHUNK)
                ys = slice(r0 + ck * SGU_CHUNK, r0 + (ck + 1) * SGU_CHUNK)
                y_scr[ys, cs] = u[rs, cs] * (_dot(ws[hd], vb[rs, cs]) + sbias_ref[:, cs])
        sgu_ref[r0:r0 + SUB_ROWS, :] = _rms(y_scr[r0:r0 + SUB_ROWS, :], og_ref[...]).astype(BF16)

    starts = list(range(0, tm, SUB_ROWS))
    ahead = [project(r0) for r0 in starts[:PIPE_DEPTH]]
    for k, r0 in enumerate(starts):
        z = ahead.pop(0)
        if k + PIPE_DEPTH < len(starts):
            ahead.append(project(starts[k + PIPE_DEPTH]))
        mix(r0, z)


def _in_proj(x, g, w_in, ln_g, ln_b, sgu_w, sgu_bias, og):
    tm = ROW_TILE
    return pl.pallas_call(
        _in_proj_kernel,
        grid=(SEQ // tm,),
        in_specs=[
            pl.BlockSpec((tm, D_MODEL), lambda i: (i, 0)),
            _const_spec((1, D_MODEL)),
            _const_spec((D_MODEL, IN_WIDTH)),
            _const_spec((1, SGU_WIDTH)),
            _const_spec((1, SGU_WIDTH)),
            _const_spec((SGU_HEADS, SGU_CHUNK, SGU_CHUNK)),
            _const_spec((SGU_CHUNK, SGU_WIDTH)),
            _const_spec((1, SGU_WIDTH)),
        ],
        out_specs=[
            pl.BlockSpec((S5_BLOCKS, tm, LANES), lambda i: (0, i, 0)),
            pl.BlockSpec((tm, SGU_WIDTH), lambda i: (i, 0)),
        ],
        out_shape=[
            jax.ShapeDtypeStruct((S5_BLOCKS, SEQ, LANES), F32),
            jax.ShapeDtypeStruct((SEQ, SGU_WIDTH), BF16),
        ],
        scratch_shapes=[pltpu.VMEM((tm, SGU_WIDTH), F32)],
        compiler_params=pltpu.CompilerParams(
            dimension_semantics=("arbitrary",), vmem_limit_bytes=VMEM_LIMIT),
        name="in_proj_sgu",
    )(x, g, w_in, ln_g, ln_b, sgu_w, sgu_bias, og)


def _s5_kernel(u_ref, rowp_ref, btr_ref, bti_ref, ctr_ref, cti_ref, d_ref,
               y_ref, ucat_scr, bx_scr, cyt_scr, k2_scr, x_scr):
    nst = S5_BLOCK_STATE

    are = rowp_ref[0, 0:1, :]
    aim = rowp_ref[0, 1:2, :]
    dt = jnp.exp(rowp_ref[0, 2:3, :])
    er = jnp.exp(are * dt)
    abr = er * jnp.cos(aim * dt)
    abi = er * jnp.sin(aim * dt)
    den = are * are + aim * aim
    cr = ((abr - 1.0) * are + abi * aim) / den
    ci = (abi * are - (abr - 1.0) * aim) / den
    btr = btr_ref[0]
    bti = bti_ref[0]
    bbr = cr * btr - ci * bti
    bbi = cr * bti + ci * btr
    ctr = ctr_ref[0]
    cti = cti_ref[0]

    grow = lax.broadcasted_iota(jnp.int32, (LANES, 2 * nst), 0) >> 4
    gcol = (lax.broadcasted_iota(jnp.int32, (LANES, 2 * nst), 1) & (nst - 1)) >> 6
    same = grow == gcol

    def expand(re, im):
        x = jnp.concatenate([re, im], axis=1)
        x = jnp.concatenate([x] * S5_GROUPS_PER_BLOCK, axis=0)
        return jnp.where(same, x, 0.0).astype(BF16)

    pr = jnp.ones_like(abr)
    pi = jnp.zeros_like(abr)
    for k in range(S5_T + 1):
        if k < S5_T:
            rs = slice((S5_T - 1 - k) * LANES, (S5_T - k) * LANES)
            bx_scr[rs, :] = expand(pr * bbr - pi * bbi, pr * bbi + pi * bbr)
        cyt_scr[k * LANES:(k + 1) * LANES, :] = expand(ctr * pr - cti * pi, -(ctr * pi + cti * pr))
        if k < S5_T:
            pr, pi = pr * abr - pi * abi, pr * abi + pi * abr
    a16r, a16i = pr, pi

    kst_f32 = _dot_nt(bx_scr[...], cyt_scr[0:LANES, :])
    kst = kst_f32.astype(BF16)
    eye = (lax.broadcasted_iota(jnp.int32, (LANES, LANES), 0)
           == lax.broadcasted_iota(jnp.int32, (LANES, LANES), 1))
    k0 = (kst_f32[(S5_T - 1) * LANES:, :] + jnp.where(eye, d_ref[0], 0.0)).astype(BF16)
    k2_scr[:, LANES:2 * LANES] = kst
    k2_scr[(S5_T - 1) * LANES:, LANES:2 * LANES] = k0
    k2_scr[0:(S5_T - 1) * LANES, 0:LANES] = kst[LANES:, :]
    k2_scr[(S5_T - 2) * LANES:(S5_T - 1) * LANES, 0:LANES] = k0
    k2_scr[(S5_T - 1) * LANES:, 0:LANES] = jnp.zeros((LANES, LANES), BF16)

    for t in range(S5_T):
        ucat_scr[:, t * LANES:(t + 1) * LANES] = (
            u_ref[0, pl.ds(t, S5_CHUNKS, stride=S5_T), :].astype(BF16))
    u = ucat_scr[...]

    x_scr[...] = _dot(u, bx_scr[...])

    y_intra = [_dot(u[:, 0:(t + 2) * LANES], k2_scr[(S5_T - 2 - t) * LANES:, :])
               for t in range(0, S5_T, 2)]

    sr = jnp.zeros((1, nst), F32)
    si = jnp.zeros((1, nst), F32)
    for c in range(S5_CHUNKS):
        xr = x_scr[c:c + 1, 0:nst]
        xi = x_scr[c:c + 1, nst:2 * nst]
        x_scr[c:c + 1, 0:nst] = sr
        x_scr[c:c + 1, nst:2 * nst] = si
        sr, si = a16r * sr - a16i * si + xr, a16r * si + a16i * sr + xi
    s_in = x_scr[...].astype(BF16)

    for k, t in enumerate(range(0, S5_T, 2)):
        y2 = y_intra[k] + _dot_nt(s_in, cyt_scr[(t + 1) * LANES:(t + 3) * LANES, :])
        for q in range(2):
            y_ref[0, pl.ds(t + q, S5_CHUNKS, stride=S5_T), :] = y2[:, q * LANES:(q + 1) * LANES]


def _s5_core(u3, rowp, btr, bti, ctr, cti, d):
    blk3 = lambda *s: pl.BlockSpec((1,) + s, lambda b: (b, 0, 0))
    return pl.pallas_call(
        _s5_kernel,
        grid=(S5_BLOCKS,),
        in_specs=[
            blk3(SEQ, LANES),
            blk3(8, S5_BLOCK_STATE),
            blk3(SSM_GROUP, S5_BLOCK_STATE),
            blk3(SSM_GROUP, S5_BLOCK_STATE),
            blk3(SSM_GROUP, S5_BLOCK_STATE),
            blk3(SSM_GROUP, S5_BLOCK_STATE),
            blk3(1, LANES),
        ],
        out_specs=blk3(SEQ, LANES),
        out_shape=jax.ShapeDtypeStruct((S5_BLOCKS, SEQ, LANES), F32),
        scratch_shapes=[
            pltpu.VMEM((S5_CHUNKS, S5_CHUNK_COLS), BF16),
            pltpu.VMEM((S5_CHUNK_COLS, 2 * S5_BLOCK_STATE), BF16),
            pltpu.VMEM(((S5_T + 1) * LANES, 2 * S5_BLOCK_STATE), BF16),
            pltpu.VMEM((S5_CHUNK_COLS, 2 * LANES), BF16),
            pltpu.VMEM((S5_CHUNKS, 2 * S5_BLOCK_STATE), F32),
        ],
        compiler_params=pltpu.CompilerParams(
            dimension_semantics=("arbitrary",), vmem_limit_bytes=VMEM_LIMIT),
        name="s5_core",
    )(u3, rowp, btr, bti, ctr, cti, d)


def _out_proj_kernel(y_ref, sgu_ref, x_ref, gluw_ref, glub_ref, g_ref, wout_ref, o_ref):
    def glu(r0):
        rs = slice(r0, r0 + SUB_ROWS)
        y = jnp.concatenate([y_ref[b, rs, :] for b in range(S5_BLOCKS)], axis=1)
        y = jax.nn.gelu(y)
        gate = jax.nn.sigmoid(_dot(y.astype(BF16), gluw_ref[...]) + glub_ref[...])
        return _rms(y * gate, g_ref[...]).astype(BF16)

    starts = list(range(0, x_ref.shape[0], SUB_ROWS))
    ahead = [glu(r0) for r0 in starts[:PIPE_DEPTH]]
    for k, r0 in enumerate(starts):
        rs = slice(r0, r0 + SUB_ROWS)
        mixed = jnp.concatenate([ahead.pop(0), sgu_ref[rs, :]], axis=1)
        proj = _dot(mixed, wout_ref[...])
        if k + PIPE_DEPTH < len(starts):
            ahead.append(glu(starts[k + PIPE_DEPTH]))
        o_ref[rs, :] = x_ref[rs, :] + proj


def _out_proj(y3, sgu_n, x, glu_w, glu_b, g, w_out):
    tm = ROW_TILE
    return pl.pallas_call(
        _out_proj_kernel,
        grid=(SEQ // tm,),
        in_specs=[
            pl.BlockSpec((S5_BLOCKS, tm, LANES), lambda i: (0, i, 0)),
            pl.BlockSpec((tm, SGU_WIDTH), lambda i: (i, 0)),
            pl.BlockSpec((tm, D_MODEL), lambda i: (i, 0)),
            _const_spec((SSM_WIDTH, SSM_WIDTH)),
            _const_spec((1, SSM_WIDTH)),
            _const_spec((1, SSM_WIDTH)),
            _const_spec((D_MODEL, D_MODEL)),
        ],
        out_specs=pl.BlockSpec((tm, D_MODEL), lambda i: (i, 0)),
        out_shape=jax.ShapeDtypeStruct((SEQ, D_MODEL), F32),
        compiler_params=pltpu.CompilerParams(
            dimension_semantics=("arbitrary",), vmem_limit_bytes=VMEM_LIMIT),
        name="s5_glu_out_proj",
    )(y3, sgu_n, x, glu_w, glu_b, g, w_out)


def _mlp_step(x_ref, g_ref, wu_ref, wd_ref, gf_ref, o_ref, h_scr, cast_weights=None):
    j = pl.program_id(1)

    @pl.when(j == 0)
    def _():
        x = x_ref[...]
        h_scr[...] = _rms(x, g_ref[...]).astype(BF16)
        o_ref[...] = x

    if cast_weights is not None:
        cast_weights()
    r = jnp.square(jnp.maximum(_dot(h_scr[...], wu_ref[...]), 0.0)).astype(BF16)
    o_ref[...] += _dot(r, wd_ref[...])

    @pl.when(j == pl.num_programs(1) - 1)
    def _():
        o_ref[...] = _rms(o_ref[...], gf_ref[...])


def _mlp_first_kernel(x_ref, g_ref, wup_ref, wdn_ref, gf_ref, o_ref, wupb_ref, wdnb_ref, h_scr):
    def cast_weights():
        wupb_ref[...] = wup_ref[...].astype(BF16)
        wdnb_ref[...] = wdn_ref[...].astype(BF16)

    _mlp_step(x_ref, g_ref, wupb_ref, wdnb_ref, gf_ref, o_ref, h_scr, cast_weights)


def _mlp_rest_kernel(x_ref, g_ref, wupb_ref, wdnb_ref, gf_ref, first_ref, o_ref, h_scr):
    del first_ref
    _mlp_step(x_ref, g_ref, wupb_ref, wdnb_ref, gf_ref, o_ref, h_scr)


def _mlp(x1, g, w_up, w_down, gf):
    tm, tf = MLP_ROW_TILE, MLP_FF_TILE
    nj = D_FF // tf
    params = pltpu.CompilerParams(
        dimension_semantics=("arbitrary", "arbitrary"), vmem_limit_bytes=VMEM_LIMIT)
    out_shape = jax.ShapeDtypeStruct((SEQ, D_MODEL), F32)
    scratch = [pltpu.VMEM((tm, D_MODEL), BF16)]
    wupb_spec = pl.BlockSpec((D_MODEL, tf), lambda i, j: (0, j))
    wdnb_spec = pl.BlockSpec((tf, D_MODEL), lambda i, j: (j, 0))

    first, wupb, wdnb = pl.pallas_call(
        _mlp_first_kernel,
        grid=(1, nj),
        in_specs=[
            pl.BlockSpec((tm, D_MODEL), lambda i, j: (0, 0), pipeline_mode=pl.Buffered(1)),
            _const_spec((1, D_MODEL)),
            pl.BlockSpec((D_MODEL, tf), lambda i, j: (0, j)),
            pl.BlockSpec((tf, D_MODEL), lambda i, j: (j, 0)),
            _const_spec((1, D_MODEL)),
        ],
        out_specs=[pl.BlockSpec((tm, D_MODEL), lambda i, j: (0, 0)), wupb_spec, wdnb_spec],
        out_shape=[out_shape,
                   jax.ShapeDtypeStruct((D_MODEL, D_FF), BF16),
                   jax.ShapeDtypeStruct((D_FF, D_MODEL), BF16)],
        scratch_shapes=scratch,
        compiler_params=params,
        name="mlp_first_tile",
    )(x1, g, w_up, w_down, gf)

    return pl.pallas_call(
        _mlp_rest_kernel,
        grid=(SEQ // tm - 1, nj),
        in_specs=[
            pl.BlockSpec((tm, D_MODEL), lambda i, j: (i + 1, 0)),
            _const_spec((1, D_MODEL)),
            wupb_spec,
            wdnb_spec,
            _const_spec((1, D_MODEL)),
            pl.BlockSpec(memory_space=pl.ANY),
        ],
        out_specs=pl.BlockSpec((tm, D_MODEL), lambda i, j: (i + 1, 0)),
        out_shape=out_shape,
        input_output_aliases={5: 0},
        scratch_shapes=scratch,
        compiler_params=params,
        name="mlp_rest_tiles",
    )(x1, g, wupb, wdnb, gf, first)


def _s5_params(a_re, a_im, b_re, b_im, c_re, c_im, d, log_dt):
    nb, gpb, p, h = S5_BLOCKS, S5_GROUPS_PER_BLOCK, SSM_STATE, SSM_GROUP
    ldt = jnp.broadcast_to(log_dt[:, None], a_re.shape)
    flat = jnp.stack([a_re, a_im, ldt], axis=0).reshape(3, nb, gpb * p)
    rowp = jnp.concatenate(
        [jnp.transpose(flat, (1, 0, 2)), jnp.zeros((nb, 5, gpb * p), F32)], axis=1)

    def b_layout(b):
        return jnp.transpose(b.reshape(nb, gpb, p, h), (0, 3, 1, 2)).reshape(nb, h, gpb * p)

    def c_layout(c):
        return jnp.transpose(c.reshape(nb, gpb, h, p), (0, 2, 1, 3)).reshape(nb, h, gpb * p)

    return (rowp, b_layout(b_re), b_layout(b_im), c_layout(c_re), c_layout(c_im),
            d.reshape(nb, 1, LANES))


def kernel(x, norm_mix_g, w_in, ssm_a_re, ssm_a_im, ssm_b_re, ssm_b_im, ssm_c_re, ssm_c_im,
           ssm_d, ssm_log_dt, ssm_glu_w, ssm_glu_b, sgu_ln_g, sgu_ln_b, sgu_w, sgu_b,
           out_norm_ssm_g, out_norm_sgu_g, w_out, norm_mlp_g, w_up, w_down, norm_final_g):
    assert norm_mix_g.shape[0] == 1, "single-layer problem: the MLP call applies the final norm"
    l = 0
    xs = x.reshape(SEQ, D_MODEL)
    row = lambda v: v.reshape(1, -1)
    sgu_bias = jnp.repeat(sgu_b[l].T, SGU_HEAD_DIM, axis=1)
    u3, sgu_n = _in_proj(xs, row(norm_mix_g[l]), w_in[l].astype(BF16),
                         row(sgu_ln_g[l]), row(sgu_ln_b[l]), sgu_w[l], sgu_bias,
                         row(out_norm_sgu_g[l]))
    s5p = _s5_params(ssm_a_re[l], ssm_a_im[l], ssm_b_re[l], ssm_b_im[l],
                     ssm_c_re[l], ssm_c_im[l], ssm_d[l], ssm_log_dt[l])
    y3 = _s5_core(u3, *s5p)
    x1 = _out_proj(y3, sgu_n, xs, ssm_glu_w[l].astype(BF16), row(ssm_glu_b[l]),
                   row(out_norm_ssm_g[l]), w_out[l].astype(BF16))
    out = _mlp(x1, row(norm_mlp_g[l]), w_up[l], w_down[l], row(norm_final_g))
    return out.reshape(x.shape)
```

```python
import jax
import jax.numpy as jnp
from jax import lax
from jax.experimental import pallas as pl
from jax.experimental.pallas import tpu as pltpu

D_MODEL = 2048
SEQ = 8192
SSM_WIDTH = 1024
SSM_GROUP = 16
SSM_STATE = 64
SGU_WIDTH = 1024
SGU_HEADS = 8
SGU_HEAD_DIM = 128
SGU_CHUNK = 128
IN_WIDTH = SSM_WIDTH + 2 * SGU_WIDTH
D_FF = 4 * D_MODEL
EPS = 1e-6

LANES = 128
S5_T = 16
S5_CHUNKS = SEQ // S5_T
S5_BLOCKS = SSM_WIDTH // LANES
S5_GROUPS_PER_BLOCK = LANES // SSM_GROUP
S5_BLOCK_STATE = S5_GROUPS_PER_BLOCK * SSM_STATE
S5_CHUNK_COLS = S5_T * LANES

ROW_TILE = 512
SUB_ROWS = 256
PIPE_DEPTH = 2
MLP_ROW_TILE = 1024
MLP_FF_TILE = 1024
VMEM_LIMIT = 60 * 1024 * 1024

F32 = jnp.float32
BF16 = jnp.bfloat16


def _rms(x, g):
    return x * lax.rsqrt(jnp.mean(x * x, axis=-1, keepdims=True) + EPS) * g


def _const_spec(shape):
    n = len(shape)
    return pl.BlockSpec(shape, lambda *_: (0,) * n, pipeline_mode=pl.Buffered(1))


def _dot(a, b):
    return jnp.dot(a, b, preferred_element_type=F32)


def _dot_nt(a, b):
    return lax.dot_general(a, b, (((1,), (1,)), ((), ())), preferred_element_type=F32)


def _in_proj_kernel(x_ref, g_ref, win_ref, lng_ref, lnb_ref, sw_ref, sbias_ref,
                    og_ref, wup_ref, wout_ref, gluw_ref,
                    u_ref, sgu_ref, wupb_ref, woutb_ref, gluwb_ref, y_scr):
    tm = x_ref.shape[0]
    wupb_ref[...] = wup_ref[...].astype(BF16)
    woutb_ref[...] = wout_ref[...].astype(BF16)
    gluwb_ref[...] = gluw_ref[...].astype(BF16)
    row = lax.broadcasted_iota(jnp.int32, (SGU_CHUNK, SGU_CHUNK), 0)
    col = lax.broadcasted_iota(jnp.int32, (SGU_CHUNK, SGU_CHUNK), 1)
    causal = row >= col
    ws = [jnp.where(causal, sw_ref[hd], 0.0).astype(BF16) for hd in range(SGU_HEADS)]

    def project(r0):
        h = _rms(x_ref[r0:r0 + SUB_ROWS, :], g_ref[...]).astype(BF16)
        return _dot(h, win_ref[...])

    def mix(r0, z):
        for b in range(S5_BLOCKS):
            u_ref[b, r0:r0 + SUB_ROWS, :] = z[:, b * LANES:(b + 1) * LANES]

        a = jax.nn.gelu(z[:, SSM_WIDTH:])
        u = a[:, :SGU_WIDTH]
        v = a[:, SGU_WIDTH:]
        mu = jnp.mean(v, axis=-1, keepdims=True)
        vc = v - mu
        var = jnp.mean(vc * vc, axis=-1, keepdims=True)
        vb = (vc * lax.rsqrt(var + EPS) * lng_ref[...] + lnb_ref[...]).astype(BF16)
        for hd in range(SGU_HEADS):
            cs = slice(hd * SGU_HEAD_DIM, (hd + 1) * SGU_HEAD_DIM)
            for ck in range(SUB_ROWS // SGU_CHUNK):
                rs = slice(ck * SGU_CHUNK, (ck + 1) * SGU_CHUNK)
                ys = slice(r0 + ck * SGU_CHUNK, r0 + (ck + 1) * SGU_CHUNK)
                y_scr[ys, cs] = u[rs, cs] * (_dot(ws[hd], vb[rs, cs]) + sbias_ref[:, cs])
        sgu_ref[r0:r0 + SUB_ROWS, :] = _rms(y_scr[r0:r0 + SUB_ROWS, :], og_ref[...]).astype(BF16)

    starts = list(range(0, tm, SUB_ROWS))
    ahead = [project(r0) for r0 in starts[:PIPE_DEPTH]]
    for k, r0 in enumerate(starts):
        z = ahead.pop(0)
        if k + PIPE_DEPTH < len(starts):
            ahead.append(project(starts[k + PIPE_DEPTH]))
        mix(r0, z)


def _slab_spec(shape, steps):
    return pl.BlockSpec((shape[0] // steps, shape[1]), lambda i: (i, 0))


def _in_proj(x, g, w_in, ln_g, ln_b, sgu_w, sgu_bias, og, w_up, w_out, glu_w):
    tm = ROW_TILE
    steps = SEQ // tm
    casts = [w_up, w_out, glu_w]
    return pl.pallas_call(
        _in_proj_kernel,
        grid=(steps,),
        in_specs=[
            pl.BlockSpec((tm, D_MODEL), lambda i: (i, 0)),
            _const_spec((1, D_MODEL)),
            _const_spec((D_MODEL, IN_WIDTH)),
            _const_spec((1, SGU_WIDTH)),
            _const_spec((1, SGU_WIDTH)),
            _const_spec((SGU_HEADS, SGU_CHUNK, SGU_CHUNK)),
            _const_spec((SGU_CHUNK, SGU_WIDTH)),
            _const_spec((1, SGU_WIDTH)),
        ] + [_slab_spec(w.shape, steps) for w in casts],
        out_specs=[
            pl.BlockSpec((S5_BLOCKS, tm, LANES), lambda i: (0, i, 0)),
            pl.BlockSpec((tm, SGU_WIDTH), lambda i: (i, 0)),
        ] + [_slab_spec(w.shape, steps) for w in casts],
        out_shape=[
            jax.ShapeDtypeStruct((S5_BLOCKS, SEQ, LANES), F32),
            jax.ShapeDtypeStruct((SEQ, SGU_WIDTH), BF16),
        ] + [jax.ShapeDtypeStruct(w.shape, BF16) for w in casts],
        scratch_shapes=[pltpu.VMEM((tm, SGU_WIDTH), F32)],
        compiler_params=pltpu.CompilerParams(
            dimension_semantics=("arbitrary",), vmem_limit_bytes=VMEM_LIMIT),
        name="in_proj_sgu",
    )(x, g, w_in, ln_g, ln_b, sgu_w, sgu_bias, og, *casts)


def _s5_kernel(u_ref, rowp_ref, btr_ref, bti_ref, ctr_ref, cti_ref, d_ref,
               y_ref, ucat_scr, bx_scr, cyt_scr, k2_scr, x_scr):
    nst = S5_BLOCK_STATE

    are = rowp_ref[0, 0:1, :]
    aim = rowp_ref[0, 1:2, :]
    dt = jnp.exp(rowp_ref[0, 2:3, :])
    er = jnp.exp(are * dt)
    abr = er * jnp.cos(aim * dt)
    abi = er * jnp.sin(aim * dt)
    den = are * are + aim * aim
    cr = ((abr - 1.0) * are + abi * aim) / den
    ci = (abi * are - (abr - 1.0) * aim) / den
    btr = btr_ref[0]
    bti = bti_ref[0]
    bbr = cr * btr - ci * bti
    bbi = cr * bti + ci * btr
    ctr = ctr_ref[0]
    cti = cti_ref[0]

    grow = lax.broadcasted_iota(jnp.int32, (LANES, 2 * nst), 0) >> 4
    gcol = (lax.broadcasted_iota(jnp.int32, (LANES, 2 * nst), 1) & (nst - 1)) >> 6
    same = grow == gcol

    def expand(re, im):
        x = jnp.concatenate([re, im], axis=1)
        x = jnp.concatenate([x] * S5_GROUPS_PER_BLOCK, axis=0)
        return jnp.where(same, x, 0.0).astype(BF16)

    pr = jnp.ones_like(abr)
    pi = jnp.zeros_like(abr)
    for k in range(S5_T + 1):
        if k < S5_T:
            rs = slice((S5_T - 1 - k) * LANES, (S5_T - k) * LANES)
            bx_scr[rs, :] = expand(pr * bbr - pi * bbi, pr * bbi + pi * bbr)
        cyt_scr[k * LANES:(k + 1) * LANES, :] = expand(ctr * pr - cti * pi, -(ctr * pi + cti * pr))
        if k < S5_T:
            pr, pi = pr * abr - pi * abi, pr * abi + pi * abr
    a16r, a16i = pr, pi

    kst_f32 = _dot_nt(bx_scr[...], cyt_scr[0:LANES, :])
    kst = kst_f32.astype(BF16)
    eye = (lax.broadcasted_iota(jnp.int32, (LANES, LANES), 0)
           == lax.broadcasted_iota(jnp.int32, (LANES, LANES), 1))
    k0 = (kst_f32[(S5_T - 1) * LANES:, :] + jnp.where(eye, d_ref[0], 0.0)).astype(BF16)
    k2_scr[:, LANES:2 * LANES] = kst
    k2_scr[(S5_T - 1) * LANES:, LANES:2 * LANES] = k0
    k2_scr[0:(S5_T - 1) * LANES, 0:LANES] = kst[LANES:, :]
    k2_scr[(S5_T - 2) * LANES:(S5_T - 1) * LANES, 0:LANES] = k0
    k2_scr[(S5_T - 1) * LANES:, 0:LANES] = jnp.zeros((LANES, LANES), BF16)

    for t in range(S5_T):
        ucat_scr[:, t * LANES:(t + 1) * LANES] = (
            u_ref[0, pl.ds(t, S5_CHUNKS, stride=S5_T), :].astype(BF16))
    u = ucat_scr[...]

    x_scr[...] = _dot(u, bx_scr[...])

    y_intra = [_dot(u[:, 0:(t + 2) * LANES], k2_scr[(S5_T - 2 - t) * LANES:, :])
               for t in range(0, S5_T, 2)]

    sr = jnp.zeros((1, nst), F32)
    si = jnp.zeros((1, nst), F32)
    for c in range(S5_CHUNKS):
        xr = x_scr[c:c + 1, 0:nst]
        xi = x_scr[c:c + 1, nst:2 * nst]
        x_scr[c:c + 1, 0:nst] = sr
        x_scr[c:c + 1, nst:2 * nst] = si
        sr, si = a16r * sr - a16i * si + xr, a16r * si + a16i * sr + xi
    s_in = x_scr[...].astype(BF16)

    for k, t in enumerate(range(0, S5_T, 2)):
        y2 = y_intra[k] + _dot_nt(s_in, cyt_scr[(t + 1) * LANES:(t + 3) * LANES, :])
        for q in range(2):
            y_ref[0, pl.ds(t + q, S5_CHUNKS, stride=S5_T), :] = y2[:, q * LANES:(q + 1) * LANES]


def _s5_core(u3, rowp, btr, bti, ctr, cti, d):
    blk3 = lambda *s: pl.BlockSpec((1,) + s, lambda b: (b, 0, 0))
    return pl.pallas_call(
        _s5_kernel,
        grid=(S5_BLOCKS,),
        in_specs=[
            blk3(SEQ, LANES),
            blk3(8, S5_BLOCK_STATE),
            blk3(SSM_GROUP, S5_BLOCK_STATE),
            blk3(SSM_GROUP, S5_BLOCK_STATE),
            blk3(SSM_GROUP, S5_BLOCK_STATE),
            blk3(SSM_GROUP, S5_BLOCK_STATE),
            blk3(1, LANES),
        ],
        out_specs=blk3(SEQ, LANES),
        out_shape=jax.ShapeDtypeStruct((S5_BLOCKS, SEQ, LANES), F32),
        scratch_shapes=[
            pltpu.VMEM((S5_CHUNKS, S5_CHUNK_COLS), BF16),
            pltpu.VMEM((S5_CHUNK_COLS, 2 * S5_BLOCK_STATE), BF16),
            pltpu.VMEM(((S5_T + 1) * LANES, 2 * S5_BLOCK_STATE), BF16),
            pltpu.VMEM((S5_CHUNK_COLS, 2 * LANES), BF16),
            pltpu.VMEM((S5_CHUNKS, 2 * S5_BLOCK_STATE), F32),
        ],
        compiler_params=pltpu.CompilerParams(
            dimension_semantics=("arbitrary",), vmem_limit_bytes=VMEM_LIMIT),
        name="s5_core",
    )(u3, rowp, btr, bti, ctr, cti, d)


def _out_proj_kernel(y_ref, sgu_ref, x_ref, gluw_ref, glub_ref, g_ref, wout_ref, gm_ref,
                     wdn_ref, o_ref, h_ref, wdnb_ref):
    wdnb_ref[...] = wdn_ref[...].astype(BF16)

    def glu(r0):
        rs = slice(r0, r0 + SUB_ROWS)
        y = jnp.concatenate([y_ref[b, rs, :] for b in range(S5_BLOCKS)], axis=1)
        y = jax.nn.gelu(y)
        gate = jax.nn.sigmoid(_dot(y.astype(BF16), gluw_ref[...]) + glub_ref[...])
        return _rms(y * gate, g_ref[...]).astype(BF16)

    starts = list(range(0, x_ref.shape[0], SUB_ROWS))
    ahead = [glu(r0) for r0 in starts[:PIPE_DEPTH]]
    for k, r0 in enumerate(starts):
        rs = slice(r0, r0 + SUB_ROWS)
        mixed = jnp.concatenate([ahead.pop(0), sgu_ref[rs, :]], axis=1)
        proj = _dot(mixed, wout_ref[...])
        if k + PIPE_DEPTH < len(starts):
            ahead.append(glu(starts[k + PIPE_DEPTH]))
        x1 = x_ref[rs, :] + proj
        o_ref[rs, :] = x1
        h_ref[rs, :] = _rms(x1, gm_ref[...]).astype(BF16)


def _out_proj(y3, sgu_n, x, glu_w, glu_b, g, w_out, g_mlp, w_down):
    tm = ROW_TILE
    steps = SEQ // tm
    return pl.pallas_call(
        _out_proj_kernel,
        grid=(steps,),
        in_specs=[
            pl.BlockSpec((S5_BLOCKS, tm, LANES), lambda i: (0, i, 0)),
            pl.BlockSpec((tm, SGU_WIDTH), lambda i: (i, 0)),
            pl.BlockSpec((tm, D_MODEL), lambda i: (i, 0)),
            _const_spec((SSM_WIDTH, SSM_WIDTH)),
            _const_spec((1, SSM_WIDTH)),
            _const_spec((1, SSM_WIDTH)),
            _const_spec((D_MODEL, D_MODEL)),
            _const_spec((1, D_MODEL)),
            _slab_spec(w_down.shape, steps),
        ],
        out_specs=[
            pl.BlockSpec((tm, D_MODEL), lambda i: (i, 0)),
            pl.BlockSpec((tm, D_MODEL), lambda i: (i, 0)),
            _slab_spec(w_down.shape, steps),
        ],
        out_shape=[
            jax.ShapeDtypeStruct((SEQ, D_MODEL), F32),
            jax.ShapeDtypeStruct((SEQ, D_MODEL), BF16),
            jax.ShapeDtypeStruct(w_down.shape, BF16),
        ],
        compiler_params=pltpu.CompilerParams(
            dimension_semantics=("arbitrary",), vmem_limit_bytes=VMEM_LIMIT),
        name="s5_glu_out_proj",
    )(y3, sgu_n, x, glu_w, glu_b, g, w_out, g_mlp, w_down)


def _mlp_kernel(h_ref, x1_ref, wu_ref, wd_ref, gf_ref, o_ref):
    j = pl.program_id(1)

    @pl.when(j == 0)
    def _():
        o_ref[...] = jnp.zeros_like(o_ref)

    r = jnp.square(jnp.maximum(_dot(h_ref[...], wu_ref[...]), 0.0)).astype(BF16)
    o_ref[...] += _dot(r, wd_ref[...])
    xr = x1_ref.shape[0]
    rows = pl.ds(pl.multiple_of(j * xr, xr), xr)
    o_ref[rows, :] += x1_ref[...]

    @pl.when(j == pl.num_programs(1) - 1)
    def _():
        o_ref[...] = _rms(o_ref[...], gf_ref[...])


def _mlp(h, x1, w_up, w_down, gf):
    tm, tf = MLP_ROW_TILE, MLP_FF_TILE
    nj = D_FF // tf
    return pl.pallas_call(
        _mlp_kernel,
        grid=(SEQ // tm, nj),
        in_specs=[
            pl.BlockSpec((tm, D_MODEL), lambda i, j: (i, 0)),
            pl.BlockSpec((tm // nj, D_MODEL), lambda i, j: (i * nj + j, 0)),
            pl.BlockSpec((D_MODEL, tf), lambda i, j: (0, j)),
            pl.BlockSpec((tf, D_MODEL), lambda i, j: (j, 0)),
            _const_spec((1, D_MODEL)),
        ],
        out_specs=pl.BlockSpec((tm, D_MODEL), lambda i, j: (i, 0)),
        out_shape=jax.ShapeDtypeStruct((SEQ, D_MODEL), F32),
        compiler_params=pltpu.CompilerParams(
            dimension_semantics=("arbitrary", "arbitrary"), vmem_limit_bytes=VMEM_LIMIT),
        name="mlp_final_norm",
    )(h, x1, w_up, w_down, gf)


def _s5_params(a_re, a_im, b_re, b_im, c_re, c_im, d, log_dt):
    nb, gpb, p, h = S5_BLOCKS, S5_GROUPS_PER_BLOCK, SSM_STATE, SSM_GROUP
    ldt = jnp.broadcast_to(log_dt[:, None], a_re.shape)
    flat = jnp.stack([a_re, a_im, ldt], axis=0).reshape(3, nb, gpb * p)
    rowp = jnp.concatenate(
        [jnp.transpose(flat, (1, 0, 2)), jnp.zeros((nb, 5, gpb * p), F32)], axis=1)

    def b_layout(b):
        return jnp.transpose(b.reshape(nb, gpb, p, h), (0, 3, 1, 2)).reshape(nb, h, gpb * p)

    def c_layout(c):
        return jnp.transpose(c.reshape(nb, gpb, h, p), (0, 2, 1, 3)).reshape(nb, h, gpb * p)

    return (rowp, b_layout(b_re), b_layout(b_im), c_layout(c_re), c_layout(c_im),
            d.reshape(nb, 1, LANES))


def kernel(x, norm_mix_g, w_in, ssm_a_re, ssm_a_im, ssm_b_re, ssm_b_im, ssm_c_re, ssm_c_im,
           ssm_d, ssm_log_dt, ssm_glu_w, ssm_glu_b, sgu_ln_g, sgu_ln_b, sgu_w, sgu_b,
           out_norm_ssm_g, out_norm_sgu_g, w_out, norm_mlp_g, w_up, w_down, norm_final_g):
    assert norm_mix_g.shape[0] == 1, "single-layer problem: the MLP call applies the final norm"
    l = 0
    xs = x.reshape(SEQ, D_MODEL)
    row = lambda v: v.reshape(1, -1)
    sgu_bias = jnp.repeat(sgu_b[l].T, SGU_HEAD_DIM, axis=1)
    u3, sgu_n, w_up_bf, w_out_bf, glu_w_bf = _in_proj(
        xs, row(norm_mix_g[l]), w_in[l].astype(BF16), row(sgu_ln_g[l]), row(sgu_ln_b[l]),
        sgu_w[l], sgu_bias, row(out_norm_sgu_g[l]), w_up[l], w_out[l], ssm_glu_w[l])
    s5p = _s5_params(ssm_a_re[l], ssm_a_im[l], ssm_b_re[l], ssm_b_im[l],
                     ssm_c_re[l], ssm_c_im[l], ssm_d[l], ssm_log_dt[l])
    y3 = _s5_core(u3, *s5p)
    x1, h, w_down_bf = _out_proj(y3, sgu_n, xs, glu_w_bf, row(ssm_glu_b[l]),
                                 row(out_norm_ssm_g[l]), w_out_bf, row(norm_mlp_g[l]), w_down[l])
    out = _mlp(h, x1, w_up_bf, w_down_bf, row(norm_final_g))
    return out.reshape(x.shape)
```

```python
import jax
import jax.numpy as jnp
from jax import lax
from jax.experimental import pallas as pl
from jax.experimental.pallas import tpu as pltpu

D_MODEL = 2048
SEQ = 8192
SSM_WIDTH = 1024
SSM_GROUP = 16
SSM_STATE = 64
SGU_WIDTH = 1024
SGU_HEADS = 8
SGU_HEAD_DIM = 128
SGU_CHUNK = 128
IN_WIDTH = SSM_WIDTH + 2 * SGU_WIDTH
D_FF = 4 * D_MODEL
EPS = 1e-6

LANES = 128
S5_T = 16
S5_CHUNKS = SEQ // S5_T
S5_BLOCKS = SSM_WIDTH // LANES
S5_GROUPS_PER_BLOCK = LANES // SSM_GROUP
S5_BLOCK_STATE = S5_GROUPS_PER_BLOCK * SSM_STATE
S5_CHUNK_COLS = S5_T * LANES

ROW_TILE = 512
SUB_ROWS = 256
PIPE_DEPTH = 2
MLP_ROW_TILE = 1024
MLP_FF_TILE = 1024
VMEM_LIMIT = 60 * 1024 * 1024

F32 = jnp.float32
BF16 = jnp.bfloat16


def _rms(x, g):
    return x * lax.rsqrt(jnp.mean(x * x, axis=-1, keepdims=True) + EPS) * g


def _const_spec(shape):
    n = len(shape)
    return pl.BlockSpec(shape, lambda *_: (0,) * n, pipeline_mode=pl.Buffered(1))


def _dot(a, b):
    return jnp.dot(a, b, preferred_element_type=F32)


def _dot_nt(a, b):
    return lax.dot_general(a, b, (((1,), (1,)), ((), ())), preferred_element_type=F32)


def _in_proj_kernel(x_ref, g_ref, win_ref, lng_ref, lnb_ref, sw_ref, sbias_ref,
                    og_ref, wup_ref, u_ref, sgu_ref, wupb_ref, y_scr):
    tm = x_ref.shape[0]
    wupb_ref[...] = wup_ref[...].astype(BF16)
    row = lax.broadcasted_iota(jnp.int32, (SGU_CHUNK, SGU_CHUNK), 0)
    col = lax.broadcasted_iota(jnp.int32, (SGU_CHUNK, SGU_CHUNK), 1)
    causal = row >= col
    ws = [jnp.where(causal, sw_ref[hd], 0.0).astype(BF16) for hd in range(SGU_HEADS)]

    def project(r0):
        h = _rms(x_ref[r0:r0 + SUB_ROWS, :], g_ref[...]).astype(BF16)
        return _dot(h, win_ref[...])

    def mix(r0, z):
        for b in range(S5_BLOCKS):
            u_ref[b, r0:r0 + SUB_ROWS, :] = z[:, b * LANES:(b + 1) * LANES]

        a = jax.nn.gelu(z[:, SSM_WIDTH:])
        u = a[:, :SGU_WIDTH]
        v = a[:, SGU_WIDTH:]
        mu = jnp.mean(v, axis=-1, keepdims=True)
        vc = v - mu
        var = jnp.mean(vc * vc, axis=-1, keepdims=True)
        vb = (vc * lax.rsqrt(var + EPS) * lng_ref[...] + lnb_ref[...]).astype(BF16)
        for hd in range(SGU_HEADS):
            cs = slice(hd * SGU_HEAD_DIM, (hd + 1) * SGU_HEAD_DIM)
            for ck in range(SUB_ROWS // SGU_CHUNK):
                rs = slice(ck * SGU_CHUNK, (ck + 1) * SGU_CHUNK)
                ys = slice(r0 + ck * SGU_CHUNK, r0 + (ck + 1) * SGU_CHUNK)
                y_scr[ys, cs] = u[rs, cs] * (_dot(ws[hd], vb[rs, cs]) + sbias_ref[:, cs])
        sgu_ref[r0:r0 + SUB_ROWS, :] = _rms(y_scr[r0:r0 + SUB_ROWS, :], og_ref[...]).astype(BF16)

    starts = list(range(0, tm, SUB_ROWS))
    ahead = [project(r0) for r0 in starts[:PIPE_DEPTH]]
    for k, r0 in enumerate(starts):
        z = ahead.pop(0)
        if k + PIPE_DEPTH < len(starts):
            ahead.append(project(starts[k + PIPE_DEPTH]))
        mix(r0, z)


def _slab_spec(shape, steps):
    return pl.BlockSpec((shape[0] // steps, shape[1]), lambda i: (i, 0))


def _in_proj(x, g, w_in, ln_g, ln_b, sgu_w, sgu_bias, og, w_up):
    tm = ROW_TILE
    steps = SEQ // tm
    casts = [w_up]
    return pl.pallas_call(
        _in_proj_kernel,
        grid=(steps,),
        in_specs=[
            pl.BlockSpec((tm, D_MODEL), lambda i: (i, 0)),
            _const_spec((1, D_MODEL)),
            _const_spec((D_MODEL, IN_WIDTH)),
            _const_spec((1, SGU_WIDTH)),
            _const_spec((1, SGU_WIDTH)),
            _const_spec((SGU_HEADS, SGU_CHUNK, SGU_CHUNK)),
            _const_spec((SGU_CHUNK, SGU_WIDTH)),
            _const_spec((1, SGU_WIDTH)),
        ] + [_slab_spec(w.shape, steps) for w in casts],
        out_specs=[
            pl.BlockSpec((S5_BLOCKS, tm, LANES), lambda i: (0, i, 0)),
            pl.BlockSpec((tm, SGU_WIDTH), lambda i: (i, 0)),
        ] + [_slab_spec(w.shape, steps) for w in casts],
        out_shape=[
            jax.ShapeDtypeStruct((S5_BLOCKS, SEQ, LANES), F32),
            jax.ShapeDtypeStruct((SEQ, SGU_WIDTH), BF16),
        ] + [jax.ShapeDtypeStruct(w.shape, BF16) for w in casts],
        scratch_shapes=[pltpu.VMEM((tm, SGU_WIDTH), F32)],
        compiler_params=pltpu.CompilerParams(
            dimension_semantics=("arbitrary",), vmem_limit_bytes=VMEM_LIMIT),
        name="in_proj_sgu",
    )(x, g, w_in, ln_g, ln_b, sgu_w, sgu_bias, og, *casts)


def _s5_kernel(u_ref, p_ref, wout_ref, gluw_ref,
               y_ref, woutb_ref, gluwb_ref, ucat_scr, bx_scr, cyt_scr, k2_scr, x_scr):
    nst = S5_BLOCK_STATE
    woutb_ref[...] = wout_ref[...].astype(BF16)
    gluwb_ref[...] = gluw_ref[...].astype(BF16)

    are = p_ref[0, 0:1, :]
    aim = p_ref[0, 1:2, :]
    dt = jnp.exp(p_ref[0, 2:3, :])
    d_row = p_ref[0, 3:4, 0:LANES]
    er = jnp.exp(are * dt)
    abr = er * jnp.cos(aim * dt)
    abi = er * jnp.sin(aim * dt)
    den = are * are + aim * aim
    cr = ((abr - 1.0) * are + abi * aim) / den
    ci = (abi * are - (abr - 1.0) * aim) / den
    g = SSM_GROUP
    btr = p_ref[0, 8:8 + g, :]
    bti = p_ref[0, 8 + g:8 + 2 * g, :]
    bbr = cr * btr - ci * bti
    bbi = cr * bti + ci * btr
    ctr = p_ref[0, 8 + 2 * g:8 + 3 * g, :]
    cti = p_ref[0, 8 + 3 * g:8 + 4 * g, :]

    grow = lax.broadcasted_iota(jnp.int32, (LANES, 2 * nst), 0) >> 4
    gcol = (lax.broadcasted_iota(jnp.int32, (LANES, 2 * nst), 1) & (nst - 1)) >> 6
    same = grow == gcol

    def expand(re, im):
        x = jnp.concatenate([re, im], axis=1)
        x = jnp.concatenate([x] * S5_GROUPS_PER_BLOCK, axis=0)
        return jnp.where(same, x, 0.0).astype(BF16)

    pr = jnp.ones_like(abr)
    pi = jnp.zeros_like(abr)
    for k in range(S5_T + 1):
        if k < S5_T:
            rs = slice((S5_T - 1 - k) * LANES, (S5_T - k) * LANES)
            bx_scr[rs, :] = expand(pr * bbr - pi * bbi, pr * bbi + pi * bbr)
        cyt_scr[k * LANES:(k + 1) * LANES, :] = expand(ctr * pr - cti * pi, -(ctr * pi + cti * pr))
        if k < S5_T:
            pr, pi = pr * abr - pi * abi, pr * abi + pi * abr
    a16r, a16i = pr, pi

    kst_f32 = _dot_nt(bx_scr[...], cyt_scr[0:LANES, :])
    kst = kst_f32.astype(BF16)
    eye = (lax.broadcasted_iota(jnp.int32, (LANES, LANES), 0)
           == lax.broadcasted_iota(jnp.int32, (LANES, LANES), 1))
    k0 = (kst_f32[(S5_T - 1) * LANES:, :] + jnp.where(eye, d_row, 0.0)).astype(BF16)
    k2_scr[:, LANES:2 * LANES] = kst
    k2_scr[(S5_T - 1) * LANES:, LANES:2 * LANES] = k0
    k2_scr[0:(S5_T - 1) * LANES, 0:LANES] = kst[LANES:, :]
    k2_scr[(S5_T - 2) * LANES:(S5_T - 1) * LANES, 0:LANES] = k0
    k2_scr[(S5_T - 1) * LANES:, 0:LANES] = jnp.zeros((LANES, LANES), BF16)

    for t in range(S5_T):
        ucat_scr[:, t * LANES:(t + 1) * LANES] = (
            u_ref[0, pl.ds(t, S5_CHUNKS, stride=S5_T), :].astype(BF16))
    u = ucat_scr[...]

    x_scr[...] = _dot(u, bx_scr[...])

    sr = jnp.zeros((1, nst), F32)
    si = jnp.zeros((1, nst), F32)
    half = S5_CHUNKS // 2
    for c0 in range(0, S5_CHUNKS, half):
        for c in range(c0, c0 + half):
            xr = x_scr[c:c + 1, 0:nst]
            xi = x_scr[c:c + 1, nst:2 * nst]
            x_scr[c:c + 1, 0:nst] = sr
            x_scr[c:c + 1, nst:2 * nst] = si
            sr, si = a16r * sr - a16i * si + xr, a16r * si + a16i * sr + xi
        s_in = x_scr[c0:c0 + half, :].astype(BF16)
        uh = u[c0:c0 + half, :]
        for t in range(0, S5_T, 2):
            y2 = _dot(uh[:, 0:(t + 2) * LANES], k2_scr[(S5_T - 2 - t) * LANES:, :])
            y2 += _dot_nt(s_in, cyt_scr[(t + 1) * LANES:(t + 3) * LANES, :])
            for q in range(2):
                rows = pl.ds(c0 * S5_T + t + q, half, stride=S5_T)
                y_ref[0, rows, :] = y2[:, q * LANES:(q + 1) * LANES]


def _s5_core(u3, params, w_out, glu_w):
    blk3 = lambda *s: pl.BlockSpec((1,) + s, lambda b: (b, 0, 0))
    casts = [w_out, glu_w]
    return pl.pallas_call(
        _s5_kernel,
        grid=(S5_BLOCKS,),
        in_specs=[blk3(SEQ, LANES), blk3(*params.shape[1:])]
        + [_slab_spec(w.shape, S5_BLOCKS) for w in casts],
        out_specs=[blk3(SEQ, LANES)] + [_slab_spec(w.shape, S5_BLOCKS) for w in casts],
        out_shape=[jax.ShapeDtypeStruct((S5_BLOCKS, SEQ, LANES), F32)]
        + [jax.ShapeDtypeStruct(w.shape, BF16) for w in casts],
        scratch_shapes=[
            pltpu.VMEM((S5_CHUNKS, S5_CHUNK_COLS), BF16),
            pltpu.VMEM((S5_CHUNK_COLS, 2 * S5_BLOCK_STATE), BF16),
            pltpu.VMEM(((S5_T + 1) * LANES, 2 * S5_BLOCK_STATE), BF16),
            pltpu.VMEM((S5_CHUNK_COLS, 2 * LANES), BF16),
            pltpu.VMEM((S5_CHUNKS, 2 * S5_BLOCK_STATE), F32),
        ],
        compiler_params=pltpu.CompilerParams(
            dimension_semantics=("arbitrary",), vmem_limit_bytes=VMEM_LIMIT),
        name="s5_core",
    )(u3, params, *casts)


def _out_proj_kernel(y_ref, sgu_ref, x_ref, gluw_ref, glub_ref, g_ref, wout_ref, gm_ref,
                     wdn_ref, o_ref, h_ref, wdnb_ref):
    wdnb_ref[...] = wdn_ref[...].astype(BF16)

    def glu(r0):
        rs = slice(r0, r0 + SUB_ROWS)
        y = jnp.concatenate([y_ref[b, rs, :] for b in range(S5_BLOCKS)], axis=1)
        y = jax.nn.gelu(y)
        gate = jax.nn.sigmoid(_dot(y.astype(BF16), gluw_ref[...]) + glub_ref[...])
        return _rms(y * gate, g_ref[...]).astype(BF16)

    starts = list(range(0, x_ref.shape[0], SUB_ROWS))
    ahead = [glu(r0) for r0 in starts[:PIPE_DEPTH]]
    for k, r0 in enumerate(starts):
        rs = slice(r0, r0 + SUB_ROWS)
        mixed = jnp.concatenate([ahead.pop(0), sgu_ref[rs, :]], axis=1)
        proj = _dot(mixed, wout_ref[...])
        if k + PIPE_DEPTH < len(starts):
            ahead.append(glu(starts[k + PIPE_DEPTH]))
        x1 = x_ref[rs, :] + proj
        o_ref[rs, :] = x1
        h_ref[rs, :] = _rms(x1, gm_ref[...]).astype(BF16)


def _out_proj(y3, sgu_n, x, glu_w, glu_b, g, w_out, g_mlp, w_down):
    tm = ROW_TILE
    steps = SEQ // tm
    return pl.pallas_call(
        _out_proj_kernel,
        grid=(steps,),
        in_specs=[
            pl.BlockSpec((S5_BLOCKS, tm, LANES), lambda i: (0, i, 0)),
            pl.BlockSpec((tm, SGU_WIDTH), lambda i: (i, 0)),
            pl.BlockSpec((tm, D_MODEL), lambda i: (i, 0)),
            _const_spec((SSM_WIDTH, SSM_WIDTH)),
            _const_spec((1, SSM_WIDTH)),
            _const_spec((1, SSM_WIDTH)),
            _const_spec((D_MODEL, D_MODEL)),
            _const_spec((1, D_MODEL)),
            _slab_spec(w_down.shape, steps),
        ],
        out_specs=[
            pl.BlockSpec((tm, D_MODEL), lambda i: (i, 0)),
            pl.BlockSpec((tm, D_MODEL), lambda i: (i, 0)),
            _slab_spec(w_down.shape, steps),
        ],
        out_shape=[
            jax.ShapeDtypeStruct((SEQ, D_MODEL), F32),
            jax.ShapeDtypeStruct((SEQ, D_MODEL), BF16),
            jax.ShapeDtypeStruct(w_down.shape, BF16),
        ],
        compiler_params=pltpu.CompilerParams(
            dimension_semantics=("arbitrary",), vmem_limit_bytes=VMEM_LIMIT),
        name="s5_glu_out_proj",
    )(y3, sgu_n, x, glu_w, glu_b, g, w_out, g_mlp, w_down)


def _mlp_kernel(h_ref, x1_ref, wu_ref, wd_ref, gf_ref, o_ref):
    j = pl.program_id(1)

    @pl.when(j == 0)
    def _():
        o_ref[...] = jnp.zeros_like(o_ref)

    r = jnp.square(jnp.maximum(_dot(h_ref[...], wu_ref[...]), 0.0)).astype(BF16)
    o_ref[...] += _dot(r, wd_ref[...])
    xr = x1_ref.shape[0]
    rows = pl.ds(pl.multiple_of(j * xr, xr), xr)
    o_ref[rows, :] += x1_ref[...]

    @pl.when(j == pl.num_programs(1) - 1)
    def _():
        o_ref[...] = _rms(o_ref[...], gf_ref[...])


def _mlp(h, x1, w_up, w_down, gf):
    tm, tf = MLP_ROW_TILE, MLP_FF_TILE
    nj = D_FF // tf
    return pl.pallas_call(
        _mlp_kernel,
        grid=(SEQ // tm, nj),
        in_specs=[
            pl.BlockSpec((tm, D_MODEL), lambda i, j: (i, 0)),
            pl.BlockSpec((tm // nj, D_MODEL), lambda i, j: (i * nj + j, 0)),
            pl.BlockSpec((D_MODEL, tf), lambda i, j: (0, j)),
            pl.BlockSpec((tf, D_MODEL), lambda i, j: (j, 0)),
            _const_spec((1, D_MODEL)),
        ],
        out_specs=pl.BlockSpec((tm, D_MODEL), lambda i, j: (i, 0)),
        out_shape=jax.ShapeDtypeStruct((SEQ, D_MODEL), F32),
        compiler_params=pltpu.CompilerParams(
            dimension_semantics=("arbitrary", "arbitrary"), vmem_limit_bytes=VMEM_LIMIT),
        name="mlp_final_norm",
    )(h, x1, w_up, w_down, gf)


def _s5_params(a_re, a_im, b_re, b_im, c_re, c_im, d, log_dt):
    nb, gpb, p, h = S5_BLOCKS, S5_GROUPS_PER_BLOCK, SSM_STATE, SSM_GROUP
    nst = gpb * p
    ldt = jnp.broadcast_to(log_dt[:, None], a_re.shape)
    d_pad = jnp.pad(d.reshape(nb, 1, LANES), ((0, 0), (0, 0), (0, nst - LANES)))
    head = jnp.concatenate(
        [a_re.reshape(nb, 1, nst), a_im.reshape(nb, 1, nst), ldt.reshape(nb, 1, nst), d_pad,
         jnp.zeros((nb, 4, nst), F32)], axis=1)

    def b_layout(b):
        return jnp.transpose(b.reshape(nb, gpb, p, h), (0, 3, 1, 2)).reshape(nb, h, nst)

    def c_layout(c):
        return jnp.transpose(c.reshape(nb, gpb, h, p), (0, 2, 1, 3)).reshape(nb, h, nst)

    return jnp.concatenate(
        [head, b_layout(b_re), b_layout(b_im), c_layout(c_re), c_layout(c_im)], axis=1)


def kernel(x, norm_mix_g, w_in, ssm_a_re, ssm_a_im, ssm_b_re, ssm_b_im, ssm_c_re, ssm_c_im,
           ssm_d, ssm_log_dt, ssm_glu_w, ssm_glu_b, sgu_ln_g, sgu_ln_b, sgu_w, sgu_b,
           out_norm_ssm_g, out_norm_sgu_g, w_out, norm_mlp_g, w_up, w_down, norm_final_g):
    assert norm_mix_g.shape[0] == 1, "single-layer problem: the MLP call applies the final norm"
    l = 0
    xs = x.reshape(SEQ, D_MODEL)
    row = lambda v: v.reshape(1, -1)
    sgu_bias = jnp.repeat(sgu_b[l].T, SGU_HEAD_DIM, axis=1)
    u3, sgu_n, w_up_bf = _in_proj(
        xs, row(norm_mix_g[l]), w_in[l].astype(BF16), row(sgu_ln_g[l]), row(sgu_ln_b[l]),
        sgu_w[l], sgu_bias, row(out_norm_sgu_g[l]), w_up[l])
    s5p = _s5_params(ssm_a_re[l], ssm_a_im[l], ssm_b_re[l], ssm_b_im[l],
                     ssm_c_re[l], ssm_c_im[l], ssm_d[l], ssm_log_dt[l])
    y3, w_out_bf, glu_w_bf = _s5_core(u3, s5p, w_out[l], ssm_glu_w[l])
    x1, h, w_down_bf = _out_proj(y3, sgu_n, xs, glu_w_bf, row(ssm_glu_b[l]),
                                 row(out_norm_ssm_g[l]), w_out_bf, row(norm_mlp_g[l]), w_down[l])
    out = _mlp(h, x1, w_up_bf, w_down_bf, row(norm_final_g))
    return out.reshape(x.shape)
```

```python
import jax
import jax.numpy as jnp
from jax import lax
from jax.experimental import pallas as pl
from jax.experimental.pallas import tpu as pltpu

D_MODEL = 2048
SEQ = 8192
SSM_WIDTH = 1024
SSM_GROUP = 16
SSM_STATE = 64
SGU_WIDTH = 1024
SGU_HEADS = 8
SGU_HEAD_DIM = 128
SGU_CHUNK = 128
IN_WIDTH = SSM_WIDTH + 2 * SGU_WIDTH
D_FF = 4 * D_MODEL
EPS = 1e-6

LANES = 128
S5_T = 8
S5_CHUNKS = SEQ // S5_T
S5_BLOCKS = SSM_WIDTH // LANES
S5_GROUPS_PER_BLOCK = LANES // SSM_GROUP
S5_BLOCK_STATE = S5_GROUPS_PER_BLOCK * SSM_STATE
S5_CHUNK_COLS = S5_T * LANES

ROW_TILE = 512
SUB_ROWS = 256
PIPE_DEPTH = 2
MLP_ROW_TILE = 1024
MLP_FF_TILE = 1024
MLP_RESID_PARTS = 2
VMEM_LIMIT = 60 * 1024 * 1024

F32 = jnp.float32
BF16 = jnp.bfloat16


def _rms(x, g):
    return x * lax.rsqrt(jnp.mean(x * x, axis=-1, keepdims=True) + EPS) * g


def _const_spec(shape):
    n = len(shape)
    return pl.BlockSpec(shape, lambda *_: (0,) * n, pipeline_mode=pl.Buffered(1))


def _dot(a, b):
    return jnp.dot(a, b, preferred_element_type=F32)


def _dot_nt(a, b):
    return lax.dot_general(a, b, (((1,), (1,)), ((), ())), preferred_element_type=F32)


def _in_proj_kernel(x_ref, g_ref, win_ref, lng_ref, lnb_ref, sw_ref, sbias_ref,
                    og_ref, wup_ref, u_ref, sgu_ref, wupb_ref, y_scr):
    tm = x_ref.shape[0]
    wupb_ref[...] = wup_ref[...].astype(BF16)
    row = lax.broadcasted_iota(jnp.int32, (SGU_CHUNK, SGU_CHUNK), 0)
    col = lax.broadcasted_iota(jnp.int32, (SGU_CHUNK, SGU_CHUNK), 1)
    causal = row >= col
    ws = [jnp.where(causal, sw_ref[hd], 0.0).astype(BF16) for hd in range(SGU_HEADS)]

    def project(r0):
        h = _rms(x_ref[r0:r0 + SUB_ROWS, :], g_ref[...]).astype(BF16)
        return _dot(h, win_ref[...])

    def mix(r0, z):
        for b in range(S5_BLOCKS):
            u_ref[b, r0:r0 + SUB_ROWS, :] = z[:, b * LANES:(b + 1) * LANES]

        a = jax.nn.gelu(z[:, SSM_WIDTH:])
        u = a[:, :SGU_WIDTH]
        v = a[:, SGU_WIDTH:]
        mu = jnp.mean(v, axis=-1, keepdims=True)
        vc = v - mu
        var = jnp.mean(vc * vc, axis=-1, keepdims=True)
        vb = (vc * lax.rsqrt(var + EPS) * lng_ref[...] + lnb_ref[...]).astype(BF16)
        for hd in range(SGU_HEADS):
            cs = slice(hd * SGU_HEAD_DIM, (hd + 1) * SGU_HEAD_DIM)
            for ck in range(SUB_ROWS // SGU_CHUNK):
                rs = slice(ck * SGU_CHUNK, (ck + 1) * SGU_CHUNK)
                ys = slice(r0 + ck * SGU_CHUNK, r0 + (ck + 1) * SGU_CHUNK)
                y_scr[ys, cs] = u[rs, cs] * (_dot(ws[hd], vb[rs, cs]) + sbias_ref[:, cs])
        sgu_ref[r0:r0 + SUB_ROWS, :] = _rms(y_scr[r0:r0 + SUB_ROWS, :], og_ref[...]).astype(BF16)

    starts = list(range(0, tm, SUB_ROWS))
    ahead = [project(r0) for r0 in starts[:PIPE_DEPTH]]
    for k, r0 in enumerate(starts):
        z = ahead.pop(0)
        if k + PIPE_DEPTH < len(starts):
            ahead.append(project(starts[k + PIPE_DEPTH]))
        mix(r0, z)


def _slab_spec(shape, steps):
    return pl.BlockSpec((shape[0] // steps, shape[1]), lambda i: (i, 0))


def _in_proj(x, g, w_in, ln_g, ln_b, sgu_w, sgu_bias, og, w_up):
    tm = ROW_TILE
    steps = SEQ // tm
    casts = [w_up]
    return pl.pallas_call(
        _in_proj_kernel,
        grid=(steps,),
        in_specs=[
            pl.BlockSpec((tm, D_MODEL), lambda i: (i, 0)),
            _const_spec((1, D_MODEL)),
            _const_spec((D_MODEL, IN_WIDTH)),
            _const_spec((1, SGU_WIDTH)),
            _const_spec((1, SGU_WIDTH)),
            _const_spec((SGU_HEADS, SGU_CHUNK, SGU_CHUNK)),
            _const_spec((SGU_CHUNK, SGU_WIDTH)),
            _const_spec((1, SGU_WIDTH)),
        ] + [_slab_spec(w.shape, steps) for w in casts],
        out_specs=[
            pl.BlockSpec((S5_BLOCKS, tm, LANES), lambda i: (0, i, 0)),
            pl.BlockSpec((tm, SGU_WIDTH), lambda i: (i, 0)),
        ] + [_slab_spec(w.shape, steps) for w in casts],
        out_shape=[
            jax.ShapeDtypeStruct((S5_BLOCKS, SEQ, LANES), F32),
            jax.ShapeDtypeStruct((SEQ, SGU_WIDTH), BF16),
        ] + [jax.ShapeDtypeStruct(w.shape, BF16) for w in casts],
        scratch_shapes=[pltpu.VMEM((tm, SGU_WIDTH), F32)],
        compiler_params=pltpu.CompilerParams(
            dimension_semantics=("arbitrary",), vmem_limit_bytes=VMEM_LIMIT),
        name="in_proj_sgu",
    )(x, g, w_in, ln_g, ln_b, sgu_w, sgu_bias, og, *casts)


def _s5_kernel(u_ref, p_ref, wout_ref, gluw_ref,
               y_ref, woutb_ref, gluwb_ref, ucat_scr, bx_scr, cyt_scr, k2_scr, x_scr):
    nst = S5_BLOCK_STATE
    woutb_ref[...] = wout_ref[...].astype(BF16)
    gluwb_ref[...] = gluw_ref[...].astype(BF16)

    are = p_ref[0, 0:1, :]
    aim = p_ref[0, 1:2, :]
    dt = jnp.exp(p_ref[0, 2:3, :])
    d_row = p_ref[0, 3:4, 0:LANES]
    er = jnp.exp(are * dt)
    abr = er * jnp.cos(aim * dt)
    abi = er * jnp.sin(aim * dt)
    den = are * are + aim * aim
    cr = ((abr - 1.0) * are + abi * aim) / den
    ci = (abi * are - (abr - 1.0) * aim) / den
    g = SSM_GROUP
    btr = p_ref[0, 8:8 + g, :]
    bti = p_ref[0, 8 + g:8 + 2 * g, :]
    bbr = cr * btr - ci * bti
    bbi = cr * bti + ci * btr
    ctr = p_ref[0, 8 + 2 * g:8 + 3 * g, :]
    cti = p_ref[0, 8 + 3 * g:8 + 4 * g, :]

    grow = lax.broadcasted_iota(jnp.int32, (LANES, 2 * nst), 0) >> 4
    gcol = (lax.broadcasted_iota(jnp.int32, (LANES, 2 * nst), 1) & (nst - 1)) >> 6
    same = grow == gcol

    def expand(re, im):
        x = jnp.concatenate([re, im], axis=1)
        x = jnp.concatenate([x] * S5_GROUPS_PER_BLOCK, axis=0)
        return jnp.where(same, x, 0.0).astype(BF16)

    pr = jnp.ones_like(abr)
    pi = jnp.zeros_like(abr)
    for k in range(S5_T + 1):
        if k < S5_T:
            rs = slice((S5_T - 1 - k) * LANES, (S5_T - k) * LANES)
            bx_scr[rs, :] = expand(pr * bbr - pi * bbi, pr * bbi + pi * bbr)
        cyt_scr[k * LANES:(k + 1) * LANES, :] = expand(ctr * pr - cti * pi, -(ctr * pi + cti * pr))
        if k < S5_T:
            pr, pi = pr * abr - pi * abi, pr * abi + pi * abr
    a16r, a16i = pr, pi

    kst_f32 = _dot_nt(bx_scr[...], cyt_scr[0:LANES, :])
    kst = kst_f32.astype(BF16)
    eye = (lax.broadcasted_iota(jnp.int32, (LANES, LANES), 0)
           == lax.broadcasted_iota(jnp.int32, (LANES, LANES), 1))
    k0 = (kst_f32[(S5_T - 1) * LANES:, :] + jnp.where(eye, d_row, 0.0)).astype(BF16)
    k2_scr[:, LANES:2 * LANES] = kst
    k2_scr[(S5_T - 1) * LANES:, LANES:2 * LANES] = k0
    k2_scr[0:(S5_T - 1) * LANES, 0:LANES] = kst[LANES:, :]
    k2_scr[(S5_T - 2) * LANES:(S5_T - 1) * LANES, 0:LANES] = k0
    k2_scr[(S5_T - 1) * LANES:, 0:LANES] = jnp.zeros((LANES, LANES), BF16)

    for t in range(S5_T):
        ucat_scr[:, t * LANES:(t + 1) * LANES] = (
            u_ref[0, pl.ds(t, S5_CHUNKS, stride=S5_T), :].astype(BF16))
    u = ucat_scr[...]

    half = S5_CHUNKS // 2
    for c0 in range(0, S5_CHUNKS, half):
        x_scr[c0:c0 + half, :] = _dot(u[c0:c0 + half, :], bx_scr[...])

    sr = jnp.zeros((1, nst), F32)
    si = jnp.zeros((1, nst), F32)
    for c0 in range(0, S5_CHUNKS, half):
        for c in range(c0, c0 + half):
            xr = x_scr[c:c + 1, 0:nst]
            xi = x_scr[c:c + 1, nst:2 * nst]
            x_scr[c:c + 1, 0:nst] = sr
            x_scr[c:c + 1, nst:2 * nst] = si
            sr, si = a16r * sr - a16i * si + xr, a16r * si + a16i * sr + xi
        s_in = x_scr[c0:c0 + half, :].astype(BF16)
        uh = u[c0:c0 + half, :]
        for t in range(0, S5_T, 2):
            y2 = _dot(uh[:, 0:(t + 2) * LANES], k2_scr[(S5_T - 2 - t) * LANES:, :])
            y2 += _dot_nt(s_in, cyt_scr[(t + 1) * LANES:(t + 3) * LANES, :])
            for q in range(2):
                rows = pl.ds(c0 * S5_T + t + q, half, stride=S5_T)
                y_ref[0, rows, :] = y2[:, q * LANES:(q + 1) * LANES]


def _s5_core(u3, params, w_out, glu_w):
    blk3 = lambda *s: pl.BlockSpec((1,) + s, lambda b: (b, 0, 0))
    casts = [w_out, glu_w]
    return pl.pallas_call(
        _s5_kernel,
        grid=(S5_BLOCKS,),
        in_specs=[blk3(SEQ, LANES), blk3(*params.shape[1:])]
        + [_slab_spec(w.shape, S5_BLOCKS) for w in casts],
        out_specs=[blk3(SEQ, LANES)] + [_slab_spec(w.shape, S5_BLOCKS) for w in casts],
        out_shape=[jax.ShapeDtypeStruct((S5_BLOCKS, SEQ, LANES), F32)]
        + [jax.ShapeDtypeStruct(w.shape, BF16) for w in casts],
        scratch_shapes=[
            pltpu.VMEM((S5_CHUNKS, S5_CHUNK_COLS), BF16),
            pltpu.VMEM((S5_CHUNK_COLS, 2 * S5_BLOCK_STATE), BF16),
            pltpu.VMEM(((S5_T + 1) * LANES, 2 * S5_BLOCK_STATE), BF16),
            pltpu.VMEM((S5_CHUNK_COLS, 2 * LANES), BF16),
            pltpu.VMEM((S5_CHUNKS, 2 * S5_BLOCK_STATE), F32),
        ],
        compiler_params=pltpu.CompilerParams(
            dimension_semantics=("arbitrary",), vmem_limit_bytes=VMEM_LIMIT),
        name="s5_core",
    )(u3, params, *casts)


def _out_proj_kernel(y_ref, sgu_ref, x_ref, gluw_ref, glub_ref, g_ref, wout_ref, gm_ref,
                     wdn_ref, o_ref, h_ref, wdnb_ref):
    wdnb_ref[...] = wdn_ref[...].astype(BF16)

    def glu(r0):
        rs = slice(r0, r0 + SUB_ROWS)
        y = jnp.concatenate([y_ref[b, rs, :] for b in range(S5_BLOCKS)], axis=1)
        y = jax.nn.gelu(y)
        gate = jax.nn.sigmoid(_dot(y.astype(BF16), gluw_ref[...]) + glub_ref[...])
        return _rms(y * gate, g_ref[...]).astype(BF16)

    starts = list(range(0, x_ref.shape[0], SUB_ROWS))
    ahead = [glu(r0) for r0 in starts[:PIPE_DEPTH]]
    for k, r0 in enumerate(starts):
        rs = slice(r0, r0 + SUB_ROWS)
        mixed = jnp.concatenate([ahead.pop(0), sgu_ref[rs, :]], axis=1)
        proj = _dot(mixed, wout_ref[...])
        if k + PIPE_DEPTH < len(starts):
            ahead.append(glu(starts[k + PIPE_DEPTH]))
        x1 = x_ref[rs, :] + proj
        o_ref[rs, :] = x1
        h_ref[rs, :] = _rms(x1, gm_ref[...]).astype(BF16)


def _out_proj(y3, sgu_n, x, glu_w, glu_b, g, w_out, g_mlp, w_down):
    tm = ROW_TILE
    steps = SEQ // tm
    return pl.pallas_call(
        _out_proj_kernel,
        grid=(steps,),
        in_specs=[
            pl.BlockSpec((S5_BLOCKS, tm, LANES), lambda i: (0, i, 0)),
            pl.BlockSpec((tm, SGU_WIDTH), lambda i: (i, 0)),
            pl.BlockSpec((tm, D_MODEL), lambda i: (i, 0)),
            _const_spec((SSM_WIDTH, SSM_WIDTH)),
            _const_spec((1, SSM_WIDTH)),
            _const_spec((1, SSM_WIDTH)),
            _const_spec((D_MODEL, D_MODEL)),
            _const_spec((1, D_MODEL)),
            _slab_spec(w_down.shape, steps),
        ],
        out_specs=[
            pl.BlockSpec((tm, D_MODEL), lambda i: (i, 0)),
            pl.BlockSpec((tm, D_MODEL), lambda i: (i, 0)),
            _slab_spec(w_down.shape, steps),
        ],
        out_shape=[
            jax.ShapeDtypeStruct((SEQ, D_MODEL), F32),
            jax.ShapeDtypeStruct((SEQ, D_MODEL), BF16),
            jax.ShapeDtypeStruct(w_down.shape, BF16),
        ],
        compiler_params=pltpu.CompilerParams(
            dimension_semantics=("arbitrary",), vmem_limit_bytes=VMEM_LIMIT),
        name="s5_glu_out_proj",
    )(y3, sgu_n, x, glu_w, glu_b, g, w_out, g_mlp, w_down)


def _mlp_kernel(h_ref, x1_ref, wu_ref, wd_ref, gf_ref, o_ref):
    j = pl.program_id(1)
    nj = pl.num_programs(1)
    tm = o_ref.shape[0]
    xr = x1_ref.shape[0]
    steps_per_part = nj // MLP_RESID_PARTS

    def contribution(rs):
        r = jnp.square(jnp.maximum(_dot(h_ref[rs, :], wu_ref[...]), 0.0)).astype(BF16)
        return _dot(r, wd_ref[...])

    @pl.when(j == 0)
    def _():
        o_ref[...] = contribution(slice(None))
        o_ref[0:xr, :] += x1_ref[...]

    @pl.when((j > 0) & (j < nj - 1))
    def _():
        o_ref[...] += contribution(slice(None))

        @pl.when(lax.rem(j, steps_per_part) == 0)
        def _():
            rows = pl.ds(pl.multiple_of((j // steps_per_part) * xr, xr), xr)
            o_ref[rows, :] += x1_ref[...]

    @pl.when(j == nj - 1)
    def _():
        starts = list(range(0, tm, SUB_ROWS))
        nxt = contribution(slice(starts[0], starts[0] + SUB_ROWS))
        for k, r0 in enumerate(starts):
            rs = slice(r0, r0 + SUB_ROWS)
            cur = nxt
            if k + 1 < len(starts):
                nxt = contribution(slice(starts[k + 1], starts[k + 1] + SUB_ROWS))
            o_ref[rs, :] = _rms(o_ref[rs, :] + cur, gf_ref[...])


def _mlp(h, x1, w_up, w_down, gf):
    tm, tf = MLP_ROW_TILE, MLP_FF_TILE
    nj = D_FF // tf
    parts = MLP_RESID_PARTS
    assert nj % parts == 0 and nj // parts >= 2, "the last step never adds a residual slab"
    return pl.pallas_call(
        _mlp_kernel,
        grid=(SEQ // tm, nj),
        in_specs=[
            pl.BlockSpec((tm, D_MODEL), lambda i, j: (i, 0)),
            pl.BlockSpec((tm // parts, D_MODEL), lambda i, j: (i * parts + j // (nj // parts), 0)),
            pl.BlockSpec((D_MODEL, tf), lambda i, j: (0, j)),
            pl.BlockSpec((tf, D_MODEL), lambda i, j: (j, 0)),
            _const_spec((1, D_MODEL)),
        ],
        out_specs=pl.BlockSpec((tm, D_MODEL), lambda i, j: (i, 0)),
        out_shape=jax.ShapeDtypeStruct((SEQ, D_MODEL), F32),
        compiler_params=pltpu.CompilerParams(
            dimension_semantics=("arbitrary", "arbitrary"), vmem_limit_bytes=VMEM_LIMIT),
        name="mlp_final_norm",
    )(h, x1, w_up, w_down, gf)


def _s5_params(a_re, a_im, b_re, b_im, c_re, c_im, d, log_dt):
    nb, gpb, p, h = S5_BLOCKS, S5_GROUPS_PER_BLOCK, SSM_STATE, SSM_GROUP
    nst = gpb * p
    ldt = jnp.broadcast_to(log_dt[:, None], a_re.shape)
    d_pad = jnp.pad(d.reshape(nb, 1, LANES), ((0, 0), (0, 0), (0, nst - LANES)))
    head = jnp.concatenate(
        [a_re.reshape(nb, 1, nst), a_im.reshape(nb, 1, nst), ldt.reshape(nb, 1, nst), d_pad,
         jnp.zeros((nb, 4, nst), F32)], axis=1)

    def b_layout(b):
        return jnp.transpose(b.reshape(nb, gpb, p, h), (0, 3, 1, 2)).reshape(nb, h, nst)

    def c_layout(c):
        return jnp.transpose(c.reshape(nb, gpb, h, p), (0, 2, 1, 3)).reshape(nb, h, nst)

    return jnp.concatenate(
        [head, b_layout(b_re), b_layout(b_im), c_layout(c_re), c_layout(c_im)], axis=1)


def kernel(x, norm_mix_g, w_in, ssm_a_re, ssm_a_im, ssm_b_re, ssm_b_im, ssm_c_re, ssm_c_im,
           ssm_d, ssm_log_dt, ssm_glu_w, ssm_glu_b, sgu_ln_g, sgu_ln_b, sgu_w, sgu_b,
           out_norm_ssm_g, out_norm_sgu_g, w_out, norm_mlp_g, w_up, w_down, norm_final_g):
    assert norm_mix_g.shape[0] == 1, "single-layer problem: the MLP call applies the final norm"
    l = 0
    xs = x.reshape(SEQ, D_MODEL)
    row = lambda v: v.reshape(1, -1)
    sgu_bias = jnp.repeat(sgu_b[l].T, SGU_HEAD_DIM, axis=1)
    u3, sgu_n, w_up_bf = _in_proj(
        xs, row(norm_mix_g[l]), w_in[l].astype(BF16), row(sgu_ln_g[l]), row(sgu_ln_b[l]),
        sgu_w[l], sgu_bias, row(out_norm_sgu_g[l]), w_up[l])
    s5p = _s5_params(ssm_a_re[l], ssm_a_im[l], ssm_b_re[l], ssm_b_im[l],
                     ssm_c_re[l], ssm_c_im[l], ssm_d[l], ssm_log_dt[l])
    y3, w_out_bf, glu_w_bf = _s5_core(u3, s5p, w_out[l], ssm_glu_w[l])
    x1, h, w_down_bf = _out_proj(y3, sgu_n, xs, glu_w_bf, row(ssm_glu_b[l]),
                                 row(out_norm_ssm_g[l]), w_out_bf, row(norm_mlp_g[l]), w_down[l])
    out = _mlp(h, x1, w_up_bf, w_down_bf, row(norm_final_g))
    return out.reshape(x.shape)
```

```python
import jax
import jax.numpy as jnp
from jax import lax
from jax.experimental import pallas as pl
from jax.experimental.pallas import tpu as pltpu

D_MODEL = 2048
SEQ = 8192
SSM_WIDTH = 1024
SSM_GROUP = 16
SSM_STATE = 64
SGU_WIDTH = 1024
SGU_HEADS = 8
SGU_HEAD_DIM = 128
SGU_CHUNK = 128
IN_WIDTH = SSM_WIDTH + 2 * SGU_WIDTH
D_FF = 4 * D_MODEL
EPS = 1e-6

LANES = 128
S5_T = 8
S5_CHUNKS = SEQ // S5_T
S5_BLOCKS = SSM_WIDTH // LANES
S5_GROUPS_PER_BLOCK = LANES // SSM_GROUP
S5_BLOCK_STATE = S5_GROUPS_PER_BLOCK * SSM_STATE
S5_CHUNK_COLS = S5_T * LANES
S5_PARTS = 4

ROW_TILE = 512
SUB_ROWS = 256
W_IN_STAGE_ROWS = 256
PIPE_DEPTH = 2
MLP_ROW_TILE = 1024
MLP_FF_TILE = 1024
MLP_RESID_PARTS = 2
VMEM_LIMIT = 60 * 1024 * 1024

F32 = jnp.float32
BF16 = jnp.bfloat16


def _rms(x, g):
    return x * lax.rsqrt(jnp.mean(x * x, axis=-1, keepdims=True) + EPS) * g


def _const_spec(shape):
    n = len(shape)
    return pl.BlockSpec(shape, lambda *_: (0,) * n, pipeline_mode=pl.Buffered(1))


def _dot(a, b):
    return jnp.dot(a, b, preferred_element_type=F32)


def _dot_nt(a, b):
    return lax.dot_general(a, b, (((1,), (1,)), ((), ())), preferred_element_type=F32)


def _in_proj_kernel(x_ref, g_ref, win_hbm, lng_ref, lnb_ref, sw_ref, sbias_ref,
                    og_ref, wup_ref, u_ref, sgu_ref, wupb_ref, y_scr, win_ref, stage, sem):
    tm = x_ref.shape[0]

    @pl.when(pl.program_id(0) == 0)
    def _():
        rows = stage.shape[1]
        chunks = win_hbm.shape[0] // rows

        def chunk_copy(k):
            return pltpu.make_async_copy(
                win_hbm.at[pl.ds(k * rows, rows), :], stage.at[k % 2], sem.at[k % 2])

        chunk_copy(0).start()
        for k in range(chunks):
            if k + 1 < chunks:
                chunk_copy(k + 1).start()
            chunk_copy(k).wait()
            win_ref[k * rows:(k + 1) * rows, :] = stage[k % 2].astype(BF16)

    wupb_ref[...] = wup_ref[...].astype(BF16)
    row = lax.broadcasted_iota(jnp.int32, (SGU_CHUNK, SGU_CHUNK), 0)
    col = lax.broadcasted_iota(jnp.int32, (SGU_CHUNK, SGU_CHUNK), 1)
    causal = row >= col
    ws = [jnp.where(causal, sw_ref[hd], 0.0).astype(BF16) for hd in range(SGU_HEADS)]

    def project(r0):
        h = _rms(x_ref[r0:r0 + SUB_ROWS, :], g_ref[...]).astype(BF16)
        zv = _dot(h, win_ref[:, SSM_WIDTH + SGU_WIDTH:])
        zu = _dot(h, win_ref[:, SSM_WIDTH:SSM_WIDTH + SGU_WIDTH])
        zs = _dot(h, win_ref[:, :SSM_WIDTH])
        return zs, zu, zv

    def mix(r0, z):
        zs, zu, zv = z
        for b in range(S5_BLOCKS):
            u_ref[b, r0:r0 + SUB_ROWS, :] = zs[:, b * LANES:(b + 1) * LANES]

        v = jax.nn.gelu(zv)
        u = jax.nn.gelu(zu)
        mu = jnp.mean(v, axis=-1, keepdims=True)
        vc = v - mu
        var = jnp.mean(vc * vc, axis=-1, keepdims=True)
        vb = (vc * lax.rsqrt(var + EPS) * lng_ref[...] + lnb_ref[...]).astype(BF16)
        for hd in range(SGU_HEADS):
            cs = slice(hd * SGU_HEAD_DIM, (hd + 1) * SGU_HEAD_DIM)
            for ck in range(SUB_ROWS // SGU_CHUNK):
                rs = slice(ck * SGU_CHUNK, (ck + 1) * SGU_CHUNK)
                ys = slice(r0 + ck * SGU_CHUNK, r0 + (ck + 1) * SGU_CHUNK)
                y_scr[ys, cs] = u[rs, cs] * (_dot(ws[hd], vb[rs, cs]) + sbias_ref[:, cs])
        sgu_ref[r0:r0 + SUB_ROWS, :] = _rms(y_scr[r0:r0 + SUB_ROWS, :], og_ref[...]).astype(BF16)

    starts = list(range(0, tm, SUB_ROWS))
    ahead = [project(r0) for r0 in starts[:PIPE_DEPTH]]
    for k, r0 in enumerate(starts):
        z = ahead.pop(0)
        if k + PIPE_DEPTH < len(starts):
            ahead.append(project(starts[k + PIPE_DEPTH]))
        mix(r0, z)


def _slab_spec(shape, steps):
    return pl.BlockSpec((shape[0] // steps, shape[1]), lambda i: (i, 0))


def _in_proj(x, g, w_in, ln_g, ln_b, sgu_w, sgu_bias, og, w_up):
    tm = ROW_TILE
    steps = SEQ // tm
    casts = [w_up]
    return pl.pallas_call(
        _in_proj_kernel,
        grid=(steps,),
        in_specs=[
            pl.BlockSpec((tm, D_MODEL), lambda i: (i, 0)),
            _const_spec((1, D_MODEL)),
            pl.BlockSpec(memory_space=pl.ANY),
            _const_spec((1, SGU_WIDTH)),
            _const_spec((1, SGU_WIDTH)),
            _const_spec((SGU_HEADS, SGU_CHUNK, SGU_CHUNK)),
            _const_spec((SGU_CHUNK, SGU_WIDTH)),
            _const_spec((1, SGU_WIDTH)),
        ] + [_slab_spec(w.shape, steps) for w in casts],
        out_specs=[
            pl.BlockSpec((S5_BLOCKS, tm, LANES), lambda i: (0, i, 0)),
            pl.BlockSpec((tm, SGU_WIDTH), lambda i: (i, 0)),
        ] + [_slab_spec(w.shape, steps) for w in casts],
        out_shape=[
            jax.ShapeDtypeStruct((S5_BLOCKS, SEQ, LANES), F32),
            jax.ShapeDtypeStruct((SEQ, SGU_WIDTH), BF16),
        ] + [jax.ShapeDtypeStruct(w.shape, BF16) for w in casts],
        scratch_shapes=[
            pltpu.VMEM((tm, SGU_WIDTH), F32),
            pltpu.VMEM((D_MODEL, IN_WIDTH), BF16),
            pltpu.VMEM((2, W_IN_STAGE_ROWS, IN_WIDTH), F32),
            pltpu.SemaphoreType.DMA((2,)),
        ],
        compiler_params=pltpu.CompilerParams(
            dimension_semantics=("arbitrary",), vmem_limit_bytes=VMEM_LIMIT),
        name="in_proj_sgu",
    )(x, g, w_in, ln_g, ln_b, sgu_w, sgu_bias, og, *casts)


def _s5_kernel(u_ref, p_ref, wout_ref, gluw_ref,
               y_ref, woutb_ref, gluwb_ref, ucat_scr, bx_scr, cyt_scr, k2_scr, x_scr):
    nst = S5_BLOCK_STATE
    woutb_ref[...] = wout_ref[...].astype(BF16)
    gluwb_ref[...] = gluw_ref[...].astype(BF16)

    are = p_ref[0, 0:1, :]
    aim = p_ref[0, 1:2, :]
    dt = jnp.exp(p_ref[0, 2:3, :])
    d_row = p_ref[0, 3:4, 0:LANES]
    er = jnp.exp(are * dt)
    abr = er * jnp.cos(aim * dt)
    abi = er * jnp.sin(aim * dt)
    den = are * are + aim * aim
    cr = ((abr - 1.0) * are + abi * aim) / den
    ci = (abi * are - (abr - 1.0) * aim) / den
    g = SSM_GROUP
    btr = p_ref[0, 8:8 + g, :]
    bti = p_ref[0, 8 + g:8 + 2 * g, :]
    bbr = cr * btr - ci * bti
    bbi = cr * bti + ci * btr
    ctr = p_ref[0, 8 + 2 * g:8 + 3 * g, :]
    cti = p_ref[0, 8 + 3 * g:8 + 4 * g, :]

    grow = lax.broadcasted_iota(jnp.int32, (LANES, 2 * nst), 0) >> 4
    gcol = (lax.broadcasted_iota(jnp.int32, (LANES, 2 * nst), 1) & (nst - 1)) >> 6
    same = grow == gcol

    def expand(re, im):
        x = jnp.concatenate([re, im], axis=1)
        x = jnp.concatenate([x] * S5_GROUPS_PER_BLOCK, axis=0)
        return jnp.where(same, x, 0.0).astype(BF16)

    pr = jnp.ones_like(abr)
    pi = jnp.zeros_like(abr)
    for k in range(S5_T + 1):
        if k < S5_T:
            rs = slice((S5_T - 1 - k) * LANES, (S5_T - k) * LANES)
            bx_scr[rs, :] = expand(pr * bbr - pi * bbi, pr * bbi + pi * bbr)
        cyt_scr[k * LANES:(k + 1) * LANES, :] = expand(ctr * pr - cti * pi, -(ctr * pi + cti * pr))
        if k < S5_T:
            pr, pi = pr * abr - pi * abi, pr * abi + pi * abr
    a16r, a16i = pr, pi

    kst_f32 = _dot_nt(bx_scr[...], cyt_scr[0:LANES, :])
    kst = kst_f32.astype(BF16)
    eye = (lax.broadcasted_iota(jnp.int32, (LANES, LANES), 0)
           == lax.broadcasted_iota(jnp.int32, (LANES, LANES), 1))
    k0 = (kst_f32[(S5_T - 1) * LANES:, :] + jnp.where(eye, d_row, 0.0)).astype(BF16)
    k2_scr[:, LANES:2 * LANES] = kst
    k2_scr[(S5_T - 1) * LANES:, LANES:2 * LANES] = k0
    k2_scr[0:(S5_T - 1) * LANES, 0:LANES] = kst[LANES:, :]
    k2_scr[(S5_T - 2) * LANES:(S5_T - 1) * LANES, 0:LANES] = k0
    k2_scr[(S5_T - 1) * LANES:, 0:LANES] = jnp.zeros((LANES, LANES), BF16)

    for t in range(S5_T):
        ucat_scr[:, t * LANES:(t + 1) * LANES] = (
            u_ref[0, pl.ds(t, S5_CHUNKS, stride=S5_T), :].astype(BF16))
    u = ucat_scr[...]

    part = S5_CHUNKS // S5_PARTS
    for c0 in range(0, S5_CHUNKS, part):
        x_scr[c0:c0 + part, :] = _dot(u[c0:c0 + part, :], bx_scr[...])

    sr = jnp.zeros((1, nst), F32)
    si = jnp.zeros((1, nst), F32)
    for c0 in range(0, S5_CHUNKS, part):
        for c in range(c0, c0 + part):
            xr = x_scr[c:c + 1, 0:nst]
            xi = x_scr[c:c + 1, nst:2 * nst]
            x_scr[c:c + 1, 0:nst] = sr
            x_scr[c:c + 1, nst:2 * nst] = si
            sr, si = a16r * sr - a16i * si + xr, a16r * si + a16i * sr + xi
        s_in = x_scr[c0:c0 + part, :].astype(BF16)
        uh = u[c0:c0 + part, :]
        for t in range(0, S5_T, 2):
            y2 = _dot(uh[:, 0:(t + 2) * LANES], k2_scr[(S5_T - 2 - t) * LANES:, :])
            y2 += _dot_nt(s_in, cyt_scr[(t + 1) * LANES:(t + 3) * LANES, :])
            for q in range(2):
                rows = pl.ds(c0 * S5_T + t + q, part, stride=S5_T)
                y_ref[0, rows, :] = y2[:, q * LANES:(q + 1) * LANES]


def _s5_core(u3, params, w_out, glu_w):
    blk3 = lambda *s: pl.BlockSpec((1,) + s, lambda b: (b, 0, 0))
    casts = [w_out, glu_w]
    return pl.pallas_call(
        _s5_kernel,
        grid=(S5_BLOCKS,),
        in_specs=[blk3(SEQ, LANES), blk3(*params.shape[1:])]
        + [_slab_spec(w.shape, S5_BLOCKS) for w in casts],
        out_specs=[blk3(SEQ, LANES)] + [_slab_spec(w.shape, S5_BLOCKS) for w in casts],
        out_shape=[jax.ShapeDtypeStruct((S5_BLOCKS, SEQ, LANES), F32)]
        + [jax.ShapeDtypeStruct(w.shape, BF16) for w in casts],
        scratch_shapes=[
            pltpu.VMEM((S5_CHUNKS, S5_CHUNK_COLS), BF16),
            pltpu.VMEM((S5_CHUNK_COLS, 2 * S5_BLOCK_STATE), BF16),
            pltpu.VMEM(((S5_T + 1) * LANES, 2 * S5_BLOCK_STATE), BF16),
            pltpu.VMEM((S5_CHUNK_COLS, 2 * LANES), BF16),
            pltpu.VMEM((S5_CHUNKS, 2 * S5_BLOCK_STATE), F32),
        ],
        compiler_params=pltpu.CompilerParams(
            dimension_semantics=("arbitrary",), vmem_limit_bytes=VMEM_LIMIT),
        name="s5_core",
    )(u3, params, *casts)


def _out_proj_kernel(y_ref, sgu_ref, x_ref, gluw_ref, glub_ref, g_ref, wout_ref, gm_ref,
                     wdn_ref, o_ref, h_ref, wdnb_ref):
    wdnb_ref[...] = wdn_ref[...].astype(BF16)

    def glu(r0):
        rs = slice(r0, r0 + SUB_ROWS)
        y = jnp.concatenate([y_ref[b, rs, :] for b in range(S5_BLOCKS)], axis=1)
        y = jax.nn.gelu(y)
        gate = jax.nn.sigmoid(_dot(y.astype(BF16), gluw_ref[...]) + glub_ref[...])
        return _rms(y * gate, g_ref[...]).astype(BF16)

    starts = list(range(0, x_ref.shape[0], SUB_ROWS))
    ahead = [glu(r0) for r0 in starts[:PIPE_DEPTH]]
    for k, r0 in enumerate(starts):
        rs = slice(r0, r0 + SUB_ROWS)
        mixed = jnp.concatenate([ahead.pop(0), sgu_ref[rs, :]], axis=1)
        proj = _dot(mixed, wout_ref[...])
        if k + PIPE_DEPTH < len(starts):
            ahead.append(glu(starts[k + PIPE_DEPTH]))
        x1 = x_ref[rs, :] + proj
        o_ref[rs, :] = x1
        h_ref[rs, :] = _rms(x1, gm_ref[...]).astype(BF16)


def _out_proj(y3, sgu_n, x, glu_w, glu_b, g, w_out, g_mlp, w_down):
    tm = ROW_TILE
    steps = SEQ // tm
    return pl.pallas_call(
        _out_proj_kernel,
        grid=(steps,),
        in_specs=[
            pl.BlockSpec((S5_BLOCKS, tm, LANES), lambda i: (0, i, 0)),
            pl.BlockSpec((tm, SGU_WIDTH), lambda i: (i, 0)),
            pl.BlockSpec((tm, D_MODEL), lambda i: (i, 0)),
            _const_spec((SSM_WIDTH, SSM_WIDTH)),
            _const_spec((1, SSM_WIDTH)),
            _const_spec((1, SSM_WIDTH)),
            _const_spec((D_MODEL, D_MODEL)),
            _const_spec((1, D_MODEL)),
            _slab_spec(w_down.shape, steps),
        ],
        out_specs=[
            pl.BlockSpec((tm, D_MODEL), lambda i: (i, 0)),
            pl.BlockSpec((tm, D_MODEL), lambda i: (i, 0)),
            _slab_spec(w_down.shape, steps),
        ],
        out_shape=[
            jax.ShapeDtypeStruct((SEQ, D_MODEL), F32),
            jax.ShapeDtypeStruct((SEQ, D_MODEL), BF16),
            jax.ShapeDtypeStruct(w_down.shape, BF16),
        ],
        compiler_params=pltpu.CompilerParams(
            dimension_semantics=("arbitrary",), vmem_limit_bytes=VMEM_LIMIT),
        name="s5_glu_out_proj",
    )(y3, sgu_n, x, glu_w, glu_b, g, w_out, g_mlp, w_down)


def _mlp_kernel(h_ref, x1_ref, wu_ref, wd_ref, gf_ref, o_ref):
    j = pl.program_id(1)
    nj = pl.num_programs(1)
    tm = o_ref.shape[0]
    xr = x1_ref.shape[0]
    steps_per_part = nj // MLP_RESID_PARTS

    def contribution(rs):
        r = jnp.square(jnp.maximum(_dot(h_ref[rs, :], wu_ref[...]), 0.0)).astype(BF16)
        return _dot(r, wd_ref[...])

    @pl.when(j == 0)
    def _():
        o_ref[...] = contribution(slice(None))
        o_ref[0:xr, :] += x1_ref[...]

    @pl.when((j > 0) & (j < nj - 1))
    def _():
        o_ref[...] += contribution(slice(None))

        @pl.when(lax.rem(j, steps_per_part) == 0)
        def _():
            rows = pl.ds(pl.multiple_of((j // steps_per_part) * xr, xr), xr)
            o_ref[rows, :] += x1_ref[...]

    @pl.when(j == nj - 1)
    def _():
        starts = list(range(0, tm, SUB_ROWS))
        nxt = contribution(slice(starts[0], starts[0] + SUB_ROWS))
        for k, r0 in enumerate(starts):
            rs = slice(r0, r0 + SUB_ROWS)
            cur = nxt
            if k + 1 < len(starts):
                nxt = contribution(slice(starts[k + 1], starts[k + 1] + SUB_ROWS))
            o_ref[rs, :] = _rms(o_ref[rs, :] + cur, gf_ref[...])


def _mlp(h, x1, w_up, w_down, gf):
    tm, tf = MLP_ROW_TILE, MLP_FF_TILE
    nj = D_FF // tf
    parts = MLP_RESID_PARTS
    assert nj % parts == 0 and nj // parts >= 2, "the last step never adds a residual slab"
    return pl.pallas_call(
        _mlp_kernel,
        grid=(SEQ // tm, nj),
        in_specs=[
            pl.BlockSpec((tm, D_MODEL), lambda i, j: (i, 0)),
            pl.BlockSpec((tm // parts, D_MODEL), lambda i, j: (i * parts + j // (nj // parts), 0)),
            pl.BlockSpec((D_MODEL, tf), lambda i, j: (0, j)),
            pl.BlockSpec((tf, D_MODEL), lambda i, j: (j, 0)),
            _const_spec((1, D_MODEL)),
        ],
        out_specs=pl.BlockSpec((tm, D_MODEL), lambda i, j: (i, 0)),
        out_shape=jax.ShapeDtypeStruct((SEQ, D_MODEL), F32),
        compiler_params=pltpu.CompilerParams(
            dimension_semantics=("arbitrary", "arbitrary"), vmem_limit_bytes=VMEM_LIMIT),
        name="mlp_final_norm",
    )(h, x1, w_up, w_down, gf)


def _s5_params(a_re, a_im, b_re, b_im, c_re, c_im, d, log_dt):
    nb, gpb, p, h = S5_BLOCKS, S5_GROUPS_PER_BLOCK, SSM_STATE, SSM_GROUP
    nst = gpb * p
    ldt = jnp.broadcast_to(log_dt[:, None], a_re.shape)
    d_pad = jnp.pad(d.reshape(nb, 1, LANES), ((0, 0), (0, 0), (0, nst - LANES)))
    head = jnp.concatenate(
        [a_re.reshape(nb, 1, nst), a_im.reshape(nb, 1, nst), ldt.reshape(nb, 1, nst), d_pad,
         jnp.zeros((nb, 4, nst), F32)], axis=1)

    def b_layout(b):
        return jnp.transpose(b.reshape(nb, gpb, p, h), (0, 3, 1, 2)).reshape(nb, h, nst)

    def c_layout(c):
        return jnp.transpose(c.reshape(nb, gpb, h, p), (0, 2, 1, 3)).reshape(nb, h, nst)

    return jnp.concatenate(
        [head, b_layout(b_re), b_layout(b_im), c_layout(c_re), c_layout(c_im)], axis=1)


def kernel(x, norm_mix_g, w_in, ssm_a_re, ssm_a_im, ssm_b_re, ssm_b_im, ssm_c_re, ssm_c_im,
           ssm_d, ssm_log_dt, ssm_glu_w, ssm_glu_b, sgu_ln_g, sgu_ln_b, sgu_w, sgu_b,
           out_norm_ssm_g, out_norm_sgu_g, w_out, norm_mlp_g, w_up, w_down, norm_final_g):
    assert norm_mix_g.shape[0] == 1, "single-layer problem: the MLP call applies the final norm"
    l = 0
    xs = x.reshape(SEQ, D_MODEL)
    row = lambda v: v.reshape(1, -1)
    sgu_bias = jnp.repeat(sgu_b[l].T, SGU_HEAD_DIM, axis=1)
    u3, sgu_n, w_up_bf = _in_proj(
        xs, row(norm_mix_g[l]), w_in[l], row(sgu_ln_g[l]), row(sgu_ln_b[l]),
        sgu_w[l], sgu_bias, row(out_norm_sgu_g[l]), w_up[l])
    s5p = _s5_params(ssm_a_re[l], ssm_a_im[l], ssm_b_re[l], ssm_b_im[l],
                     ssm_c_re[l], ssm_c_im[l], ssm_d[l], ssm_log_dt[l])
    y3, w_out_bf, glu_w_bf = _s5_core(u3, s5p, w_out[l], ssm_glu_w[l])
    x1, h, w_down_bf = _out_proj(y3, sgu_n, xs, glu_w_bf, row(ssm_glu_b[l]),
                                 row(out_norm_ssm_g[l]), w_out_bf, row(norm_mlp_g[l]), w_down[l])
    out = _mlp(h, x1, w_up_bf, w_down_bf, row(norm_final_g))
    return out.reshape(x.shape)
```

```python
import jax
import jax.numpy as jnp
from jax import lax
from jax.experimental import pallas as pl
from jax.experimental.pallas import tpu as pltpu

D_MODEL = 2048
SEQ = 8192
SSM_WIDTH = 1024
SSM_GROUP = 16
SSM_STATE = 64
SGU_WIDTH = 1024
SGU_HEADS = 8
SGU_HEAD_DIM = 128
SGU_CHUNK = 128
IN_WIDTH = SSM_WIDTH + 2 * SGU_WIDTH
D_FF = 4 * D_MODEL
EPS = 1e-6

LANES = 128
S5_T = 8
S5_CHUNKS = SEQ // S5_T
S5_BLOCKS = SSM_WIDTH // LANES
S5_GROUPS_PER_BLOCK = LANES // SSM_GROUP
S5_BLOCK_STATE = S5_GROUPS_PER_BLOCK * SSM_STATE
S5_CHUNK_COLS = S5_T * LANES
S5_PARTS = 4

ROW_TILE = 512
SUB_ROWS = 256
W_IN_STAGE_ROWS = 256
PIPE_DEPTH = 2
MLP_ROW_TILE = 1024
MLP_FF_TILE = 1024
V7X_VMEM_BYTES = 64 * 1024 * 1024
VMEM_LIMIT = V7X_VMEM_BYTES - 4 * 1024 * 1024

F32 = jnp.float32
BF16 = jnp.bfloat16


def _rms(x, g):
    return x * lax.rsqrt(jnp.mean(x * x, axis=-1, keepdims=True) + EPS) * g


def _const_spec(shape):
    n = len(shape)
    return pl.BlockSpec(shape, lambda *_: (0,) * n, pipeline_mode=pl.Buffered(1))


def _dot(a, b):
    return jnp.dot(a, b, preferred_element_type=F32)


def _dot_nt(a, b):
    return lax.dot_general(a, b, (((1,), (1,)), ((), ())), preferred_element_type=F32)


def _in_proj_kernel(x_ref, g_ref, win_hbm, lng_ref, lnb_ref, sw_ref, sbias_ref,
                    og_ref, wup_ref, u_ref, sgu_ref, wupb_ref, y_scr, win_ref, stage, sem):
    tm = x_ref.shape[0]

    @pl.when(pl.program_id(0) == 0)
    def _():
        rows = stage.shape[1]
        chunks = win_hbm.shape[0] // rows

        def chunk_copy(k):
            return pltpu.make_async_copy(
                win_hbm.at[pl.ds(k * rows, rows), :], stage.at[k % 2], sem.at[k % 2])

        chunk_copy(0).start()
        for k in range(chunks):
            if k + 1 < chunks:
                chunk_copy(k + 1).start()
            chunk_copy(k).wait()
            win_ref[k * rows:(k + 1) * rows, :] = stage[k % 2].astype(BF16)

    wupb_ref[...] = wup_ref[...].astype(BF16)
    row = lax.broadcasted_iota(jnp.int32, (SGU_CHUNK, SGU_CHUNK), 0)
    col = lax.broadcasted_iota(jnp.int32, (SGU_CHUNK, SGU_CHUNK), 1)
    causal = row >= col
    ws = [jnp.where(causal, sw_ref[hd], 0.0).astype(BF16) for hd in range(SGU_HEADS)]

    def project(r0):
        h = _rms(x_ref[r0:r0 + SUB_ROWS, :], g_ref[...]).astype(BF16)
        zv = _dot(h, win_ref[:, SSM_WIDTH + SGU_WIDTH:])
        zu = _dot(h, win_ref[:, SSM_WIDTH:SSM_WIDTH + SGU_WIDTH])
        zs = _dot(h, win_ref[:, :SSM_WIDTH])
        return zs, zu, zv

    def mix(r0, z):
        zs, zu, zv = z
        for b in range(S5_BLOCKS):
            u_ref[b, r0:r0 + SUB_ROWS, :] = zs[:, b * LANES:(b + 1) * LANES]

        v = jax.nn.gelu(zv)
        u = jax.nn.gelu(zu)
        mu = jnp.mean(v, axis=-1, keepdims=True)
        vc = v - mu
        var = jnp.mean(vc * vc, axis=-1, keepdims=True)
        vb = (vc * lax.rsqrt(var + EPS) * lng_ref[...] + lnb_ref[...]).astype(BF16)
        for hd in range(SGU_HEADS):
            cs = slice(hd * SGU_HEAD_DIM, (hd + 1) * SGU_HEAD_DIM)
            for ck in range(SUB_ROWS // SGU_CHUNK):
                rs = slice(ck * SGU_CHUNK, (ck + 1) * SGU_CHUNK)
                ys = slice(r0 + ck * SGU_CHUNK, r0 + (ck + 1) * SGU_CHUNK)
                y_scr[ys, cs] = u[rs, cs] * (_dot(ws[hd], vb[rs, cs]) + sbias_ref[:, cs])
        sgu_ref[r0:r0 + SUB_ROWS, :] = _rms(y_scr[r0:r0 + SUB_ROWS, :], og_ref[...]).astype(BF16)

    starts = list(range(0, tm, SUB_ROWS))
    ahead = [project(r0) for r0 in starts[:PIPE_DEPTH]]
    for k, r0 in enumerate(starts):
        z = ahead.pop(0)
        if k + PIPE_DEPTH < len(starts):
            ahead.append(project(starts[k + PIPE_DEPTH]))
        mix(r0, z)


def _slab_spec(shape, steps):
    return pl.BlockSpec((shape[0] // steps, shape[1]), lambda i: (i, 0))


def _in_proj(x, g, w_in, ln_g, ln_b, sgu_w, sgu_bias, og, w_up):
    tm = ROW_TILE
    steps = SEQ // tm
    casts = [w_up]
    return pl.pallas_call(
        _in_proj_kernel,
        grid=(steps,),
        in_specs=[
            pl.BlockSpec((tm, D_MODEL), lambda i: (i, 0)),
            _const_spec((1, D_MODEL)),
            pl.BlockSpec(memory_space=pl.ANY),
            _const_spec((1, SGU_WIDTH)),
            _const_spec((1, SGU_WIDTH)),
            _const_spec((SGU_HEADS, SGU_CHUNK, SGU_CHUNK)),
            _const_spec((SGU_CHUNK, SGU_WIDTH)),
            _const_spec((1, SGU_WIDTH)),
        ] + [_slab_spec(w.shape, steps) for w in casts],
        out_specs=[
            pl.BlockSpec((S5_BLOCKS, tm, LANES), lambda i: (0, i, 0)),
            pl.BlockSpec((tm, SGU_WIDTH), lambda i: (i, 0)),
        ] + [_slab_spec(w.shape, steps) for w in casts],
        out_shape=[
            jax.ShapeDtypeStruct((S5_BLOCKS, SEQ, LANES), F32),
            jax.ShapeDtypeStruct((SEQ, SGU_WIDTH), BF16),
        ] + [jax.ShapeDtypeStruct(w.shape, BF16) for w in casts],
        scratch_shapes=[
            pltpu.VMEM((tm, SGU_WIDTH), F32),
            pltpu.VMEM((D_MODEL, IN_WIDTH), BF16),
            pltpu.VMEM((2, W_IN_STAGE_ROWS, IN_WIDTH), F32),
            pltpu.SemaphoreType.DMA((2,)),
        ],
        compiler_params=pltpu.CompilerParams(
            dimension_semantics=("arbitrary",), vmem_limit_bytes=VMEM_LIMIT),
        name="in_proj_sgu",
    )(x, g, w_in, ln_g, ln_b, sgu_w, sgu_bias, og, *casts)


def _s5_kernel(u_ref, p_ref, wout_ref, gluw_ref,
               y_ref, woutb_ref, gluwb_ref, ucat_scr, bx_scr, cyt_scr, k2_scr, x_scr):
    nst = S5_BLOCK_STATE
    woutb_ref[...] = wout_ref[...].astype(BF16)
    gluwb_ref[...] = gluw_ref[...].astype(BF16)

    are = p_ref[0, 0:1, :]
    aim = p_ref[0, 1:2, :]
    dt = jnp.exp(p_ref[0, 2:3, :])
    d_row = p_ref[0, 3:4, 0:LANES]
    er = jnp.exp(are * dt)
    abr = er * jnp.cos(aim * dt)
    abi = er * jnp.sin(aim * dt)
    den = are * are + aim * aim
    cr = ((abr - 1.0) * are + abi * aim) / den
    ci = (abi * are - (abr - 1.0) * aim) / den
    g = SSM_GROUP
    btr = p_ref[0, 8:8 + g, :]
    bti = p_ref[0, 8 + g:8 + 2 * g, :]
    bbr = cr * btr - ci * bti
    bbi = cr * bti + ci * btr
    ctr = p_ref[0, 8 + 2 * g:8 + 3 * g, :]
    cti = p_ref[0, 8 + 3 * g:8 + 4 * g, :]

    grow = lax.broadcasted_iota(jnp.int32, (LANES, 2 * nst), 0) >> 4
    gcol = (lax.broadcasted_iota(jnp.int32, (LANES, 2 * nst), 1) & (nst - 1)) >> 6
    same = grow == gcol

    def expand(re, im):
        x = jnp.concatenate([re, im], axis=1)
        x = jnp.concatenate([x] * S5_GROUPS_PER_BLOCK, axis=0)
        return jnp.where(same, x, 0.0).astype(BF16)

    pr = jnp.ones_like(abr)
    pi = jnp.zeros_like(abr)
    for k in range(S5_T + 1):
        if k < S5_T:
            rs = slice((S5_T - 1 - k) * LANES, (S5_T - k) * LANES)
            bx_scr[rs, :] = expand(pr * bbr - pi * bbi, pr * bbi + pi * bbr)
        cyt_scr[k * LANES:(k + 1) * LANES, :] = expand(ctr * pr - cti * pi, -(ctr * pi + cti * pr))
        if k < S5_T:
            pr, pi = pr * abr - pi * abi, pr * abi + pi * abr
    at_r, at_i = pr, pi

    kst_f32 = _dot_nt(bx_scr[...], cyt_scr[0:LANES, :])
    kst = kst_f32.astype(BF16)
    eye = (lax.broadcasted_iota(jnp.int32, (LANES, LANES), 0)
           == lax.broadcasted_iota(jnp.int32, (LANES, LANES), 1))
    k0 = (kst_f32[(S5_T - 1) * LANES:, :] + jnp.where(eye, d_row, 0.0)).astype(BF16)
    k2_scr[:, LANES:2 * LANES] = kst
    k2_scr[(S5_T - 1) * LANES:, LANES:2 * LANES] = k0
    k2_scr[0:(S5_T - 1) * LANES, 0:LANES] = kst[LANES:, :]
    k2_scr[(S5_T - 2) * LANES:(S5_T - 1) * LANES, 0:LANES] = k0
    k2_scr[(S5_T - 1) * LANES:, 0:LANES] = jnp.zeros((LANES, LANES), BF16)

    for t in range(S5_T):
        ucat_scr[:, t * LANES:(t + 1) * LANES] = (
            u_ref[0, pl.ds(t, S5_CHUNKS, stride=S5_T), :].astype(BF16))
    u = ucat_scr[...]

    part = S5_CHUNKS // S5_PARTS
    for c0 in range(0, S5_CHUNKS, part):
        x_scr[c0:c0 + part, :] = _dot(u[c0:c0 + part, :], bx_scr[...])

    sr = jnp.zeros((1, nst), F32)
    si = jnp.zeros((1, nst), F32)
    for c0 in range(0, S5_CHUNKS, part):
        for c in range(c0, c0 + part):
            xr = x_scr[c:c + 1, 0:nst]
            xi = x_scr[c:c + 1, nst:2 * nst]
            x_scr[c:c + 1, 0:nst] = sr
            x_scr[c:c + 1, nst:2 * nst] = si
            sr, si = at_r * sr - at_i * si + xr, at_r * si + at_i * sr + xi
        s_in = x_scr[c0:c0 + part, :].astype(BF16)
        uh = u[c0:c0 + part, :]
        for t in range(0, S5_T, 2):
            y2 = _dot(uh[:, 0:(t + 2) * LANES], k2_scr[(S5_T - 2 - t) * LANES:, :])
            y2 += _dot_nt(s_in, cyt_scr[(t + 1) * LANES:(t + 3) * LANES, :])
            for q in range(2):
                rows = pl.ds(c0 * S5_T + t + q, part, stride=S5_T)
                y_ref[0, rows, :] = y2[:, q * LANES:(q + 1) * LANES]


def _s5_core(u3, params, w_out, glu_w):
    blk3 = lambda *s: pl.BlockSpec((1,) + s, lambda b: (b, 0, 0))
    casts = [w_out, glu_w]
    return pl.pallas_call(
        _s5_kernel,
        grid=(S5_BLOCKS,),
        in_specs=[blk3(SEQ, LANES), blk3(*params.shape[1:])]
        + [_slab_spec(w.shape, S5_BLOCKS) for w in casts],
        out_specs=[blk3(SEQ, LANES)] + [_slab_spec(w.shape, S5_BLOCKS) for w in casts],
        out_shape=[jax.ShapeDtypeStruct((S5_BLOCKS, SEQ, LANES), F32)]
        + [jax.ShapeDtypeStruct(w.shape, BF16) for w in casts],
        scratch_shapes=[
            pltpu.VMEM((S5_CHUNKS, S5_CHUNK_COLS), BF16),
            pltpu.VMEM((S5_CHUNK_COLS, 2 * S5_BLOCK_STATE), BF16),
            pltpu.VMEM(((S5_T + 1) * LANES, 2 * S5_BLOCK_STATE), BF16),
            pltpu.VMEM((S5_CHUNK_COLS, 2 * LANES), BF16),
            pltpu.VMEM((S5_CHUNKS, 2 * S5_BLOCK_STATE), F32),
        ],
        compiler_params=pltpu.CompilerParams(
            dimension_semantics=("arbitrary",), vmem_limit_bytes=VMEM_LIMIT),
        name="s5_core",
    )(u3, params, *casts)


def _out_proj_kernel(y_ref, sgu_ref, x_ref, gluw_ref, glub_ref, g_ref, wout_ref, gm_ref,
                     wdn_ref, o_ref, h_ref, wdnb_ref):
    def glu(r0):
        rs = slice(r0, r0 + SUB_ROWS)
        y = jnp.concatenate([y_ref[b, rs, :] for b in range(S5_BLOCKS)], axis=1)
        y = jax.nn.gelu(y)
        gate = jax.nn.sigmoid(_dot(y.astype(BF16), gluw_ref[...]) + glub_ref[...])
        return _rms(y * gate, g_ref[...]).astype(BF16)

    starts = list(range(0, x_ref.shape[0], SUB_ROWS))
    gated = [glu(r0) for r0 in starts]
    projs = [_dot(jnp.concatenate([a_n, sgu_ref[r0:r0 + SUB_ROWS, :]], axis=1), wout_ref[...])
             for a_n, r0 in zip(gated, starts)]
    for proj, r0 in zip(projs, starts):
        rs = slice(r0, r0 + SUB_ROWS)
        x1 = x_ref[rs, :] + proj
        o_ref[rs, :] = x1
        h_ref[rs, :] = _rms(x1, gm_ref[...]).astype(BF16)
    wdnb_ref[...] = wdn_ref[...].astype(BF16)


def _out_proj(y3, sgu_n, x, glu_w, glu_b, g, w_out, g_mlp, w_down):
    tm = ROW_TILE
    steps = SEQ // tm
    return pl.pallas_call(
        _out_proj_kernel,
        grid=(steps,),
        in_specs=[
            pl.BlockSpec((S5_BLOCKS, tm, LANES), lambda i: (0, i, 0)),
            pl.BlockSpec((tm, SGU_WIDTH), lambda i: (i, 0)),
            pl.BlockSpec((tm, D_MODEL), lambda i: (i, 0)),
            _const_spec((SSM_WIDTH, SSM_WIDTH)),
            _const_spec((1, SSM_WIDTH)),
            _const_spec((1, SSM_WIDTH)),
            _const_spec((D_MODEL, D_MODEL)),
            _const_spec((1, D_MODEL)),
            _slab_spec(w_down.shape, steps),
        ],
        out_specs=[
            pl.BlockSpec((tm, D_MODEL), lambda i: (i, 0)),
            pl.BlockSpec((tm, D_MODEL), lambda i: (i, 0)),
            _slab_spec(w_down.shape, steps),
        ],
        out_shape=[
            jax.ShapeDtypeStruct((SEQ, D_MODEL), F32),
            jax.ShapeDtypeStruct((SEQ, D_MODEL), BF16),
            jax.ShapeDtypeStruct(w_down.shape, BF16),
        ],
        compiler_params=pltpu.CompilerParams(
            dimension_semantics=("arbitrary",), vmem_limit_bytes=VMEM_LIMIT),
        name="s5_glu_out_proj",
    )(y3, sgu_n, x, glu_w, glu_b, g, w_out, g_mlp, w_down)


def _mlp_kernel(h_ref, x1_ref, wu_ref, wd_ref, gf_ref, o_ref):
    j = pl.program_id(1)
    nj = pl.num_programs(1)
    tm = o_ref.shape[0]
    xr = x1_ref.shape[0]

    def contribution(rs):
        r = jnp.square(jnp.maximum(_dot(h_ref[rs, :], wu_ref[...]), 0.0)).astype(BF16)
        return _dot(r, wd_ref[...])

    @pl.when(j == 0)
    def _():
        o_ref[...] = contribution(slice(None))
        o_ref[0:xr, :] += x1_ref[...]

    @pl.when((j > 0) & (j < nj - 1))
    def _():
        o_ref[...] += contribution(slice(None))
        rows = pl.ds(pl.multiple_of(j * xr, xr), xr)
        o_ref[rows, :] += x1_ref[...]

    @pl.when(j == nj - 1)
    def _():
        o_ref[tm - xr:tm, :] += x1_ref[...]
        starts = list(range(0, tm, SUB_ROWS))
        nxt = contribution(slice(starts[0], starts[0] + SUB_ROWS))
        for k, r0 in enumerate(starts):
            rs = slice(r0, r0 + SUB_ROWS)
            cur = nxt
            if k + 1 < len(starts):
                nxt = contribution(slice(starts[k + 1], starts[k + 1] + SUB_ROWS))
            o_ref[rs, :] = _rms(o_ref[rs, :] + cur, gf_ref[...])


def _mlp(h, x1, w_up, w_down, gf):
    tm, tf = MLP_ROW_TILE, MLP_FF_TILE
    nj = D_FF // tf
    return pl.pallas_call(
        _mlp_kernel,
        grid=(SEQ // tm, nj),
        in_specs=[
            pl.BlockSpec((tm, D_MODEL), lambda i, j: (i, 0)),
            pl.BlockSpec((tm // nj, D_MODEL), lambda i, j: (i * nj + j, 0)),
            pl.BlockSpec((D_MODEL, tf), lambda i, j: (0, j)),
            pl.BlockSpec((tf, D_MODEL), lambda i, j: (j, 0)),
            _const_spec((1, D_MODEL)),
        ],
        out_specs=pl.BlockSpec((tm, D_MODEL), lambda i, j: (i, 0)),
        out_shape=jax.ShapeDtypeStruct((SEQ, D_MODEL), F32),
        compiler_params=pltpu.CompilerParams(
            dimension_semantics=("arbitrary", "arbitrary"), vmem_limit_bytes=VMEM_LIMIT),
        name="mlp_final_norm",
    )(h, x1, w_up, w_down, gf)


def _s5_params(a_re, a_im, b_re, b_im, c_re, c_im, d, log_dt):
    nb, gpb, p, h = S5_BLOCKS, S5_GROUPS_PER_BLOCK, SSM_STATE, SSM_GROUP
    nst = gpb * p
    ldt = jnp.broadcast_to(log_dt[:, None], a_re.shape)
    d_pad = jnp.pad(d.reshape(nb, 1, LANES), ((0, 0), (0, 0), (0, nst - LANES)))
    head = jnp.concatenate(
        [a_re.reshape(nb, 1, nst), a_im.reshape(nb, 1, nst), ldt.reshape(nb, 1, nst), d_pad,
         jnp.zeros((nb, 4, nst), F32)], axis=1)

    def b_layout(b):
        return jnp.transpose(b.reshape(nb, gpb, p, h), (0, 3, 1, 2)).reshape(nb, h, nst)

    def c_layout(c):
        return jnp.transpose(c.reshape(nb, gpb, h, p), (0, 2, 1, 3)).reshape(nb, h, nst)

    return jnp.concatenate(
        [head, b_layout(b_re), b_layout(b_im), c_layout(c_re), c_layout(c_im)], axis=1)


def kernel(x, norm_mix_g, w_in, ssm_a_re, ssm_a_im, ssm_b_re, ssm_b_im, ssm_c_re, ssm_c_im,
           ssm_d, ssm_log_dt, ssm_glu_w, ssm_glu_b, sgu_ln_g, sgu_ln_b, sgu_w, sgu_b,
           out_norm_ssm_g, out_norm_sgu_g, w_out, norm_mlp_g, w_up, w_down, norm_final_g):
    assert norm_mix_g.shape[0] == 1, "single-layer problem: the MLP call applies the final norm"
    l = 0
    xs = x.reshape(SEQ, D_MODEL)
    row = lambda v: v.reshape(1, -1)
    sgu_bias = jnp.repeat(sgu_b[l].T, SGU_HEAD_DIM, axis=1)
    u3, sgu_n, w_up_bf = _in_proj(
        xs, row(norm_mix_g[l]), w_in[l], row(sgu_ln_g[l]), row(sgu_ln_b[l]),
        sgu_w[l], sgu_bias, row(out_norm_sgu_g[l]), w_up[l])
    s5p = _s5_params(ssm_a_re[l], ssm_a_im[l], ssm_b_re[l], ssm_b_im[l],
                     ssm_c_re[l], ssm_c_im[l], ssm_d[l], ssm_log_dt[l])
    y3, w_out_bf, glu_w_bf = _s5_core(u3, s5p, w_out[l], ssm_glu_w[l])
    x1, h, w_down_bf = _out_proj(y3, sgu_n, xs, glu_w_bf, row(ssm_glu_b[l]),
                                 row(out_norm_ssm_g[l]), w_out_bf, row(norm_mlp_g[l]), w_down[l])
    out = _mlp(h, x1, w_up_bf, w_down_bf, row(norm_final_g))
    return out.reshape(x.shape)
```

```python
import jax
import jax.numpy as jnp
from jax import lax
from jax.experimental import pallas as pl
from jax.experimental.pallas import tpu as pltpu

D_MODEL = 2048
SEQ = 8192
SSM_WIDTH = 1024
SSM_GROUP = 16
SSM_STATE = 64
SGU_WIDTH = 1024
SGU_HEADS = 8
SGU_HEAD_DIM = 128
SGU_CHUNK = 128
IN_WIDTH = SSM_WIDTH + 2 * SGU_WIDTH
D_FF = 4 * D_MODEL
EPS = 1e-6

LANES = 128
S5_T = 8
S5_CHUNKS = SEQ // S5_T
S5_BLOCKS = SSM_WIDTH // LANES
S5_GROUPS_PER_BLOCK = LANES // SSM_GROUP
S5_BLOCK_STATE = S5_GROUPS_PER_BLOCK * SSM_STATE
S5_CHUNK_COLS = S5_T * LANES
S5_PARTS = 4

ROW_TILE = 512
SUB_ROWS = 256
W_IN_STAGE_ROWS = 128
W_IN_STAGE_SLOTS = 4
PIPE_DEPTH = 2
MLP_ROW_TILE = 1024
MLP_FF_TILE = 1024
V7X_VMEM_BYTES = 64 * 1024 * 1024
VMEM_LIMIT = V7X_VMEM_BYTES - 4 * 1024 * 1024

F32 = jnp.float32
BF16 = jnp.bfloat16


def _rms(x, g):
    return x * lax.rsqrt(jnp.mean(x * x, axis=-1, keepdims=True) + EPS) * g


def _const_spec(shape):
    n = len(shape)
    return pl.BlockSpec(shape, lambda *_: (0,) * n, pipeline_mode=pl.Buffered(1))


def _dot(a, b):
    return jnp.dot(a, b, preferred_element_type=F32)


def _dot_nt(a, b):
    return lax.dot_general(a, b, (((1,), (1,)), ((), ())), preferred_element_type=F32)


def _in_proj_kernel(x_ref, g_ref, win_hbm, lng_ref, lnb_ref, sw_ref, sbias_ref,
                    og_ref, wup_ref, u_ref, sgu_ref, wupb_ref, y_scr, win_ref, stage, sem):
    tm = x_ref.shape[0]

    first_step = pl.program_id(0) == 0
    slots, rows = stage.shape[0], stage.shape[1]
    chunks = win_hbm.shape[0] // rows

    def chunk_copy(k):
        return pltpu.make_async_copy(
            win_hbm.at[pl.ds(k * rows, rows), :], stage.at[k % slots], sem.at[k % slots])

    @pl.when(first_step)
    def _():
        for k in range(slots - 1):
            chunk_copy(k).start()

    wupb_ref[...] = wup_ref[...].astype(BF16)

    @pl.when(first_step)
    def _():
        for k in range(chunks):
            if k + slots - 1 < chunks:
                chunk_copy(k + slots - 1).start()
            chunk_copy(k).wait()
            win_ref[k * rows:(k + 1) * rows, :] = stage[k % slots].astype(BF16)

    row = lax.broadcasted_iota(jnp.int32, (SGU_CHUNK, SGU_CHUNK), 0)
    col = lax.broadcasted_iota(jnp.int32, (SGU_CHUNK, SGU_CHUNK), 1)
    causal = row >= col
    ws = [jnp.where(causal, sw_ref[hd], 0.0).astype(BF16) for hd in range(SGU_HEADS)]

    def project(r0):
        h = _rms(x_ref[r0:r0 + SUB_ROWS, :], g_ref[...]).astype(BF16)
        zv = _dot(h, win_ref[:, SSM_WIDTH + SGU_WIDTH:])
        zu = _dot(h, win_ref[:, SSM_WIDTH:SSM_WIDTH + SGU_WIDTH])
        zs = _dot(h, win_ref[:, :SSM_WIDTH])
        return zs, zu, zv

    def mix(r0, z):
        zs, zu, zv = z
        for b in range(S5_BLOCKS):
            u_ref[b, r0:r0 + SUB_ROWS, :] = zs[:, b * LANES:(b + 1) * LANES]

        v = jax.nn.gelu(zv)
        u = jax.nn.gelu(zu)
        mu = jnp.mean(v, axis=-1, keepdims=True)
        vc = v - mu
        var = jnp.mean(vc * vc, axis=-1, keepdims=True)
        vb = (vc * lax.rsqrt(var + EPS) * lng_ref[...] + lnb_ref[...]).astype(BF16)
        for hd in range(SGU_HEADS):
            cs = slice(hd * SGU_HEAD_DIM, (hd + 1) * SGU_HEAD_DIM)
            for ck in range(SUB_ROWS // SGU_CHUNK):
                rs = slice(ck * SGU_CHUNK, (ck + 1) * SGU_CHUNK)
                ys = slice(r0 + ck * SGU_CHUNK, r0 + (ck + 1) * SGU_CHUNK)
                y_scr[ys, cs] = u[rs, cs] * (_dot(ws[hd], vb[rs, cs]) + sbias_ref[:, cs])
        sgu_ref[r0:r0 + SUB_ROWS, :] = _rms(y_scr[r0:r0 + SUB_ROWS, :], og_ref[...]).astype(BF16)

    starts = list(range(0, tm, SUB_ROWS))
    ahead = [project(r0) for r0 in starts[:PIPE_DEPTH]]
    for k, r0 in enumerate(starts):
        z = ahead.pop(0)
        if k + PIPE_DEPTH < len(starts):
            ahead.append(project(starts[k + PIPE_DEPTH]))
        mix(r0, z)


def _slab_spec(shape, steps):
    return pl.BlockSpec((shape[0] // steps, shape[1]), lambda i: (i, 0))


def _in_proj(x, g, w_in, ln_g, ln_b, sgu_w, sgu_bias, og, w_up):
    tm = ROW_TILE
    steps = SEQ // tm
    casts = [w_up]
    return pl.pallas_call(
        _in_proj_kernel,
        grid=(steps,),
        in_specs=[
            pl.BlockSpec((tm, D_MODEL), lambda i: (i, 0)),
            _const_spec((1, D_MODEL)),
            pl.BlockSpec(memory_space=pl.ANY),
            _const_spec((1, SGU_WIDTH)),
            _const_spec((1, SGU_WIDTH)),
            _const_spec((SGU_HEADS, SGU_CHUNK, SGU_CHUNK)),
            _const_spec((SGU_CHUNK, SGU_WIDTH)),
            _const_spec((1, SGU_WIDTH)),
        ] + [_slab_spec(w.shape, steps) for w in casts],
        out_specs=[
            pl.BlockSpec((S5_BLOCKS, tm, LANES), lambda i: (0, i, 0)),
            pl.BlockSpec((tm, SGU_WIDTH), lambda i: (i, 0)),
        ] + [_slab_spec(w.shape, steps) for w in casts],
        out_shape=[
            jax.ShapeDtypeStruct((S5_BLOCKS, SEQ, LANES), F32),
            jax.ShapeDtypeStruct((SEQ, SGU_WIDTH), BF16),
        ] + [jax.ShapeDtypeStruct(w.shape, BF16) for w in casts],
        scratch_shapes=[
            pltpu.VMEM((tm, SGU_WIDTH), F32),
            pltpu.VMEM((D_MODEL, IN_WIDTH), BF16),
            pltpu.VMEM((W_IN_STAGE_SLOTS, W_IN_STAGE_ROWS, IN_WIDTH), F32),
            pltpu.SemaphoreType.DMA((W_IN_STAGE_SLOTS,)),
        ],
        compiler_params=pltpu.CompilerParams(
            dimension_semantics=("arbitrary",), vmem_limit_bytes=VMEM_LIMIT),
        name="in_proj_sgu",
    )(x, g, w_in, ln_g, ln_b, sgu_w, sgu_bias, og, *casts)


def _s5_kernel(u_ref, p_ref, wout_ref, gluw_ref,
               y_ref, woutb_ref, gluwb_ref, ucat_scr, bx_scr, cyt_scr, k2_scr, x_scr):
    nst = S5_BLOCK_STATE
    woutb_ref[...] = wout_ref[...].astype(BF16)
    gluwb_ref[...] = gluw_ref[...].astype(BF16)

    are = p_ref[0, 0:1, :]
    aim = p_ref[0, 1:2, :]
    dt = jnp.exp(p_ref[0, 2:3, :])
    d_row = p_ref[0, 3:4, 0:LANES]
    er = jnp.exp(are * dt)
    abr = er * jnp.cos(aim * dt)
    abi = er * jnp.sin(aim * dt)
    den = are * are + aim * aim
    cr = ((abr - 1.0) * are + abi * aim) / den
    ci = (abi * are - (abr - 1.0) * aim) / den
    g = SSM_GROUP
    btr = p_ref[0, 8:8 + g, :]
    bti = p_ref[0, 8 + g:8 + 2 * g, :]
    bbr = cr * btr - ci * bti
    bbi = cr * bti + ci * btr
    ctr = p_ref[0, 8 + 2 * g:8 + 3 * g, :]
    cti = p_ref[0, 8 + 3 * g:8 + 4 * g, :]

    grow = lax.broadcasted_iota(jnp.int32, (LANES, 2 * nst), 0) >> 4
    gcol = (lax.broadcasted_iota(jnp.int32, (LANES, 2 * nst), 1) & (nst - 1)) >> 6
    same = grow == gcol

    def expand(re, im):
        x = jnp.concatenate([re, im], axis=1)
        x = jnp.concatenate([x] * S5_GROUPS_PER_BLOCK, axis=0)
        return jnp.where(same, x, 0.0).astype(BF16)

    pr = jnp.ones_like(abr)
    pi = jnp.zeros_like(abr)
    for k in range(S5_T + 1):
        if k < S5_T:
            rs = slice((S5_T - 1 - k) * LANES, (S5_T - k) * LANES)
            bx_scr[rs, :] = expand(pr * bbr - pi * bbi, pr * bbi + pi * bbr)
        cyt_scr[k * LANES:(k + 1) * LANES, :] = expand(ctr * pr - cti * pi, -(ctr * pi + cti * pr))
        if k < S5_T:
            pr, pi = pr * abr - pi * abi, pr * abi + pi * abr
    at_r, at_i = pr, pi

    kst_f32 = _dot_nt(bx_scr[...], cyt_scr[0:LANES, :])
    kst = kst_f32.astype(BF16)
    eye = (lax.broadcasted_iota(jnp.int32, (LANES, LANES), 0)
           == lax.broadcasted_iota(jnp.int32, (LANES, LANES), 1))
    k0 = (kst_f32[(S5_T - 1) * LANES:, :] + jnp.where(eye, d_row, 0.0)).astype(BF16)
    k2_scr[:, LANES:2 * LANES] = kst
    k2_scr[(S5_T - 1) * LANES:, LANES:2 * LANES] = k0
    k2_scr[0:(S5_T - 1) * LANES, 0:LANES] = kst[LANES:, :]
    k2_scr[(S5_T - 2) * LANES:(S5_T - 1) * LANES, 0:LANES] = k0
    k2_scr[(S5_T - 1) * LANES:, 0:LANES] = jnp.zeros((LANES, LANES), BF16)

    for t in range(S5_T):
        ucat_scr[:, t * LANES:(t + 1) * LANES] = (
            u_ref[0, pl.ds(t, S5_CHUNKS, stride=S5_T), :].astype(BF16))
    u = ucat_scr[...]

    part = S5_CHUNKS // S5_PARTS
    for c0 in range(0, S5_CHUNKS, part):
        x_scr[c0:c0 + part, :] = _dot(u[c0:c0 + part, :], bx_scr[...])

    sr = jnp.zeros((1, nst), F32)
    si = jnp.zeros((1, nst), F32)
    for c0 in range(0, S5_CHUNKS, part):
        for c in range(c0, c0 + part):
            xr = x_scr[c:c + 1, 0:nst]
            xi = x_scr[c:c + 1, nst:2 * nst]
            x_scr[c:c + 1, 0:nst] = sr
            x_scr[c:c + 1, nst:2 * nst] = si
            sr, si = at_r * sr - at_i * si + xr, at_r * si + at_i * sr + xi
        s_in = x_scr[c0:c0 + part, :].astype(BF16)
        uh = u[c0:c0 + part, :]
        for t in range(0, S5_T, 2):
            y2 = _dot(uh[:, 0:(t + 2) * LANES], k2_scr[(S5_T - 2 - t) * LANES:, :])
            y2 += _dot_nt(s_in, cyt_scr[(t + 1) * LANES:(t + 3) * LANES, :])
            for q in range(2):
                rows = pl.ds(c0 * S5_T + t + q, part, stride=S5_T)
                y_ref[0, rows, :] = y2[:, q * LANES:(q + 1) * LANES]


def _s5_core(u3, params, w_out, glu_w):
    blk3 = lambda *s: pl.BlockSpec((1,) + s, lambda b: (b, 0, 0))
    casts = [w_out, glu_w]
    return pl.pallas_call(
        _s5_kernel,
        grid=(S5_BLOCKS,),
        in_specs=[blk3(SEQ, LANES), blk3(*params.shape[1:])]
        + [_slab_spec(w.shape, S5_BLOCKS) for w in casts],
        out_specs=[blk3(SEQ, LANES)] + [_slab_spec(w.shape, S5_BLOCKS) for w in casts],
        out_shape=[jax.ShapeDtypeStruct((S5_BLOCKS, SEQ, LANES), F32)]
        + [jax.ShapeDtypeStruct(w.shape, BF16) for w in casts],
        scratch_shapes=[
            pltpu.VMEM((S5_CHUNKS, S5_CHUNK_COLS), BF16),
            pltpu.VMEM((S5_CHUNK_COLS, 2 * S5_BLOCK_STATE), BF16),
            pltpu.VMEM(((S5_T + 1) * LANES, 2 * S5_BLOCK_STATE), BF16),
            pltpu.VMEM((S5_CHUNK_COLS, 2 * LANES), BF16),
            pltpu.VMEM((S5_CHUNKS, 2 * S5_BLOCK_STATE), F32),
        ],
        compiler_params=pltpu.CompilerParams(
            dimension_semantics=("arbitrary",), vmem_limit_bytes=VMEM_LIMIT),
        name="s5_core",
    )(u3, params, *casts)


def _out_proj_kernel(y_ref, sgu_ref, x_ref, gluw_ref, glub_ref, g_ref, wout_ref, gm_ref,
                     wdn_ref, o_ref, h_ref, wdnb_ref):
    def glu(r0):
        rs = slice(r0, r0 + SUB_ROWS)
        y = jnp.concatenate([y_ref[b, rs, :] for b in range(S5_BLOCKS)], axis=1)
        y = jax.nn.gelu(y)
        gate = jax.nn.sigmoid(_dot(y.astype(BF16), gluw_ref[...]) + glub_ref[...])
        return _rms(y * gate, g_ref[...]).astype(BF16)

    starts = list(range(0, x_ref.shape[0], SUB_ROWS))
    gated = [glu(r0) for r0 in starts]
    projs = [_dot(jnp.concatenate([a_n, sgu_ref[r0:r0 + SUB_ROWS, :]], axis=1), wout_ref[...])
             for a_n, r0 in zip(gated, starts)]
    for proj, r0 in zip(projs, starts):
        rs = slice(r0, r0 + SUB_ROWS)
        x1 = x_ref[rs, :] + proj
        o_ref[rs, :] = x1
        h_ref[rs, :] = _rms(x1, gm_ref[...]).astype(BF16)
    wdnb_ref[...] = wdn_ref[...].astype(BF16)


def _out_proj(y3, sgu_n, x, glu_w, glu_b, g, w_out, g_mlp, w_down):
    tm = ROW_TILE
    steps = SEQ // tm
    return pl.pallas_call(
        _out_proj_kernel,
        grid=(steps,),
        in_specs=[
            pl.BlockSpec((S5_BLOCKS, tm, LANES), lambda i: (0, i, 0)),
            pl.BlockSpec((tm, SGU_WIDTH), lambda i: (i, 0)),
            pl.BlockSpec((tm, D_MODEL), lambda i: (i, 0)),
            _const_spec((SSM_WIDTH, SSM_WIDTH)),
            _const_spec((1, SSM_WIDTH)),
            _const_spec((1, SSM_WIDTH)),
            _const_spec((D_MODEL, D_MODEL)),
            _const_spec((1, D_MODEL)),
            _slab_spec(w_down.shape, steps),
        ],
        out_specs=[
            pl.BlockSpec((tm, D_MODEL), lambda i: (i, 0)),
            pl.BlockSpec((tm, D_MODEL), lambda i: (i, 0)),
            _slab_spec(w_down.shape, steps),
        ],
        out_shape=[
            jax.ShapeDtypeStruct((SEQ, D_MODEL), F32),
            jax.ShapeDtypeStruct((SEQ, D_MODEL), BF16),
            jax.ShapeDtypeStruct(w_down.shape, BF16),
        ],
        compiler_params=pltpu.CompilerParams(
            dimension_semantics=("arbitrary",), vmem_limit_bytes=VMEM_LIMIT),
        name="s5_glu_out_proj",
    )(y3, sgu_n, x, glu_w, glu_b, g, w_out, g_mlp, w_down)


def _mlp_kernel(h_ref, x1_ref, wu_ref, wd_ref, gf_ref, o_ref):
    j = pl.program_id(1)
    nj = pl.num_programs(1)
    tm = o_ref.shape[0]
    xr = x1_ref.shape[0]

    def contribution(rs):
        r = jnp.square(jnp.maximum(_dot(h_ref[rs, :], wu_ref[...]), 0.0)).astype(BF16)
        return _dot(r, wd_ref[...])

    @pl.when(j == 0)
    def _():
        o_ref[...] = contribution(slice(None))
        o_ref[0:xr, :] += x1_ref[...]

    @pl.when((j > 0) & (j < nj - 1))
    def _():
        o_ref[...] += contribution(slice(None))
        rows = pl.ds(pl.multiple_of(j * xr, xr), xr)
        o_ref[rows, :] += x1_ref[...]

    @pl.when(j == nj - 1)
    def _():
        o_ref[tm - xr:tm, :] += x1_ref[...]
        starts = list(range(0, tm, SUB_ROWS))
        nxt = contribution(slice(starts[0], starts[0] + SUB_ROWS))
        for k, r0 in enumerate(starts):
            rs = slice(r0, r0 + SUB_ROWS)
            cur = nxt
            if k + 1 < len(starts):
                nxt = contribution(slice(starts[k + 1], starts[k + 1] + SUB_ROWS))
            o_ref[rs, :] = _rms(o_ref[rs, :] + cur, gf_ref[...])


def _mlp(h, x1, w_up, w_down, gf):
    tm, tf = MLP_ROW_TILE, MLP_FF_TILE
    nj = D_FF // tf
    return pl.pallas_call(
        _mlp_kernel,
        grid=(SEQ // tm, nj),
        in_specs=[
            pl.BlockSpec((tm, D_MODEL), lambda i, j: (i, 0)),
            pl.BlockSpec((tm // nj, D_MODEL), lambda i, j: (i * nj + j, 0)),
            pl.BlockSpec((D_MODEL, tf), lambda i, j: (0, j)),
            pl.BlockSpec((tf, D_MODEL), lambda i, j: (j, 0)),
            _const_spec((1, D_MODEL)),
        ],
        out_specs=pl.BlockSpec((tm, D_MODEL), lambda i, j: (i, 0)),
        out_shape=jax.ShapeDtypeStruct((SEQ, D_MODEL), F32),
        compiler_params=pltpu.CompilerParams(
            dimension_semantics=("arbitrary", "arbitrary"), vmem_limit_bytes=VMEM_LIMIT),
        name="mlp_final_norm",
    )(h, x1, w_up, w_down, gf)


def _s5_params(a_re, a_im, b_re, b_im, c_re, c_im, d, log_dt):
    nb, gpb, p, h = S5_BLOCKS, S5_GROUPS_PER_BLOCK, SSM_STATE, SSM_GROUP
    nst = gpb * p
    ldt = jnp.broadcast_to(log_dt[:, None], a_re.shape)
    d_pad = jnp.pad(d.reshape(nb, 1, LANES), ((0, 0), (0, 0), (0, nst - LANES)))
    head = jnp.concatenate(
        [a_re.reshape(nb, 1, nst), a_im.reshape(nb, 1, nst), ldt.reshape(nb, 1, nst), d_pad,
         jnp.zeros((nb, 4, nst), F32)], axis=1)

    def b_layout(b):
        return jnp.transpose(b.reshape(nb, gpb, p, h), (0, 3, 1, 2)).reshape(nb, h, nst)

    def c_layout(c):
        return jnp.transpose(c.reshape(nb, gpb, h, p), (0, 2, 1, 3)).reshape(nb, h, nst)

    return jnp.concatenate(
        [head, b_layout(b_re), b_layout(b_im), c_layout(c_re), c_layout(c_im)], axis=1)


def kernel(x, norm_mix_g, w_in, ssm_a_re, ssm_a_im, ssm_b_re, ssm_b_im, ssm_c_re, ssm_c_im,
           ssm_d, ssm_log_dt, ssm_glu_w, ssm_glu_b, sgu_ln_g, sgu_ln_b, sgu_w, sgu_b,
           out_norm_ssm_g, out_norm_sgu_g, w_out, norm_mlp_g, w_up, w_down, norm_final_g):
    assert norm_mix_g.shape[0] == 1, "single-layer problem: the MLP call applies the final norm"
    l = 0
    xs = x.reshape(SEQ, D_MODEL)
    row = lambda v: v.reshape(1, -1)
    sgu_bias = jnp.repeat(sgu_b[l].T, SGU_HEAD_DIM, axis=1)
    u3, sgu_n, w_up_bf = _in_proj(
        xs, row(norm_mix_g[l]), w_in[l], row(sgu_ln_g[l]), row(sgu_ln_b[l]),
        sgu_w[l], sgu_bias, row(out_norm_sgu_g[l]), w_up[l])
    s5p = _s5_params(ssm_a_re[l], ssm_a_im[l], ssm_b_re[l], ssm_b_im[l],
                     ssm_c_re[l], ssm_c_im[l], ssm_d[l], ssm_log_dt[l])
    y3, w_out_bf, glu_w_bf = _s5_core(u3, s5p, w_out[l], ssm_glu_w[l])
    x1, h, w_down_bf = _out_proj(y3, sgu_n, xs, glu_w_bf, row(ssm_glu_b[l]),
                                 row(out_norm_ssm_g[l]), w_out_bf, row(norm_mlp_g[l]), w_down[l])
    out = _mlp(h, x1, w_up_bf, w_down_bf, row(norm_final_g))
    return out.reshape(x.shape)
```

```python
import jax
import jax.numpy as jnp
from jax import lax
from jax.experimental import pallas as pl
from jax.experimental.pallas import tpu as pltpu

D_MODEL = 2048
SEQ = 8192
SSM_WIDTH = 1024
SSM_GROUP = 16
SSM_STATE = 64
SGU_WIDTH = 1024
SGU_HEADS = 8
SGU_HEAD_DIM = 128
SGU_CHUNK = 128
IN_WIDTH = SSM_WIDTH + 2 * SGU_WIDTH
D_FF = 4 * D_MODEL
EPS = 1e-6

LANES = 128
S5_T = 8
S5_CHUNKS = SEQ // S5_T
S5_BLOCKS = SSM_WIDTH // LANES
S5_GROUPS_PER_BLOCK = LANES // SSM_GROUP
S5_BLOCK_STATE = S5_GROUPS_PER_BLOCK * SSM_STATE
S5_CHUNK_COLS = S5_T * LANES
S5_PARTS = 4

ROW_TILE = 512
SUB_ROWS = 256
W_IN_STAGE_ROWS = 128
W_IN_STAGE_SLOTS = 4
PIPE_DEPTH = 2
MLP_ROW_TILE = 1024
MLP_FF_TILE = 1024
V7X_VMEM_BYTES = 64 * 1024 * 1024
VMEM_LIMIT = V7X_VMEM_BYTES - 4 * 1024 * 1024

F32 = jnp.float32
BF16 = jnp.bfloat16


def _rms(x, g):
    return x * lax.rsqrt(jnp.mean(x * x, axis=-1, keepdims=True) + EPS) * g


def _const_spec(shape):
    n = len(shape)
    return pl.BlockSpec(shape, lambda *_: (0,) * n, pipeline_mode=pl.Buffered(1))


def _dot(a, b):
    return jnp.dot(a, b, preferred_element_type=F32)


def _dot_nt(a, b):
    return lax.dot_general(a, b, (((1,), (1,)), ((), ())), preferred_element_type=F32)


def _in_proj_kernel(x_ref, g_ref, win_hbm, lng_ref, lnb_ref, sw_ref, sb_ref,
                    og_ref, wup_ref, u_ref, sgu_ref, wupb_ref, y_scr, win_ref, stage, sem):
    tm = x_ref.shape[0]

    first_step = pl.program_id(0) == 0
    slots, rows = stage.shape[0], stage.shape[1]
    chunks = win_hbm.shape[0] // rows

    def chunk_copy(k):
        return pltpu.make_async_copy(
            win_hbm.at[pl.ds(k * rows, rows), :], stage.at[k % slots], sem.at[k % slots])

    @pl.when(first_step)
    def _():
        for k in range(slots - 1):
            chunk_copy(k).start()

    wupb_ref[...] = wup_ref[...].astype(BF16)

    @pl.when(first_step)
    def _():
        for k in range(chunks):
            if k + slots - 1 < chunks:
                chunk_copy(k + slots - 1).start()
            chunk_copy(k).wait()
            win_ref[k * rows:(k + 1) * rows, :] = stage[k % slots].astype(BF16)

    row = lax.broadcasted_iota(jnp.int32, (SGU_CHUNK, SGU_CHUNK), 0)
    col = lax.broadcasted_iota(jnp.int32, (SGU_CHUNK, SGU_CHUNK), 1)
    causal = row >= col
    ws = [jnp.where(causal, sw_ref[hd], 0.0).astype(BF16) for hd in range(SGU_HEADS)]
    sb_t = sb_ref[...].T

    def project(r0):
        h = _rms(x_ref[r0:r0 + SUB_ROWS, :], g_ref[...]).astype(BF16)
        zv = _dot(h, win_ref[:, SSM_WIDTH + SGU_WIDTH:])
        zu = _dot(h, win_ref[:, SSM_WIDTH:SSM_WIDTH + SGU_WIDTH])
        zs = _dot(h, win_ref[:, :SSM_WIDTH])
        return zs, zu, zv

    def mix(r0, z):
        zs, zu, zv = z
        for b in range(S5_BLOCKS):
            u_ref[b, r0:r0 + SUB_ROWS, :] = zs[:, b * LANES:(b + 1) * LANES]

        v = jax.nn.gelu(zv)
        u = jax.nn.gelu(zu)
        mu = jnp.mean(v, axis=-1, keepdims=True)
        vc = v - mu
        var = jnp.mean(vc * vc, axis=-1, keepdims=True)
        vb = (vc * lax.rsqrt(var + EPS) * lng_ref[...] + lnb_ref[...]).astype(BF16)
        for hd in range(SGU_HEADS):
            cs = slice(hd * SGU_HEAD_DIM, (hd + 1) * SGU_HEAD_DIM)
            for ck in range(SUB_ROWS // SGU_CHUNK):
                rs = slice(ck * SGU_CHUNK, (ck + 1) * SGU_CHUNK)
                ys = slice(r0 + ck * SGU_CHUNK, r0 + (ck + 1) * SGU_CHUNK)
                y_scr[ys, cs] = u[rs, cs] * (_dot(ws[hd], vb[rs, cs]) + sb_t[:, hd:hd + 1])
        sgu_ref[r0:r0 + SUB_ROWS, :] = _rms(y_scr[r0:r0 + SUB_ROWS, :], og_ref[...]).astype(BF16)

    starts = list(range(0, tm, SUB_ROWS))
    ahead = [project(r0) for r0 in starts[:PIPE_DEPTH]]
    for k, r0 in enumerate(starts):
        z = ahead.pop(0)
        if k + PIPE_DEPTH < len(starts):
            ahead.append(project(starts[k + PIPE_DEPTH]))
        mix(r0, z)


def _slab_spec(shape, steps):
    return pl.BlockSpec((shape[0] // steps, shape[1]), lambda i: (i, 0))


def _in_proj(x, g, w_in, ln_g, ln_b, sgu_w, sgu_b, og, w_up):
    tm = ROW_TILE
    steps = SEQ // tm
    casts = [w_up]
    return pl.pallas_call(
        _in_proj_kernel,
        grid=(steps,),
        in_specs=[
            pl.BlockSpec((tm, D_MODEL), lambda i: (i, 0)),
            _const_spec((1, D_MODEL)),
            pl.BlockSpec(memory_space=pl.ANY),
            _const_spec((1, SGU_WIDTH)),
            _const_spec((1, SGU_WIDTH)),
            _const_spec((SGU_HEADS, SGU_CHUNK, SGU_CHUNK)),
            _const_spec((SGU_HEADS, SGU_CHUNK)),
            _const_spec((1, SGU_WIDTH)),
        ] + [_slab_spec(w.shape, steps) for w in casts],
        out_specs=[
            pl.BlockSpec((S5_BLOCKS, tm, LANES), lambda i: (0, i, 0)),
            pl.BlockSpec((tm, SGU_WIDTH), lambda i: (i, 0)),
        ] + [_slab_spec(w.shape, steps) for w in casts],
        out_shape=[
            jax.ShapeDtypeStruct((S5_BLOCKS, SEQ, LANES), F32),
            jax.ShapeDtypeStruct((SEQ, SGU_WIDTH), BF16),
        ] + [jax.ShapeDtypeStruct(w.shape, BF16) for w in casts],
        scratch_shapes=[
            pltpu.VMEM((tm, SGU_WIDTH), F32),
            pltpu.VMEM((D_MODEL, IN_WIDTH), BF16),
            pltpu.VMEM((W_IN_STAGE_SLOTS, W_IN_STAGE_ROWS, IN_WIDTH), F32),
            pltpu.SemaphoreType.DMA((W_IN_STAGE_SLOTS,)),
        ],
        compiler_params=pltpu.CompilerParams(
            dimension_semantics=("arbitrary",), vmem_limit_bytes=VMEM_LIMIT),
        name="in_proj_sgu",
    )(x, g, w_in, ln_g, ln_b, sgu_w, sgu_b, og, *casts)


def _s5_kernel(u_ref, btr_ref, bti_ref, are_ref, aim_ref, ldt_ref, d_ref, cre_ref, cim_ref,
               wout_ref, gluw_ref,
               y_ref, woutb_ref, gluwb_ref, ucat_scr, bx_scr, cyt_scr, k2_scr, x_scr, row_scr):
    nst = S5_BLOCK_STATE
    gpb = S5_GROUPS_PER_BLOCK
    g0 = pl.program_id(0) * gpb

    def side_by_side(rows):
        return jnp.concatenate([rows[gi:gi + 1, :] for gi in range(gpb)], axis=1)

    woutb_ref[...] = wout_ref[...].astype(BF16)
    gluwb_ref[...] = gluw_ref[...].astype(BF16)

    row_scr[0:1, :] = side_by_side(are_ref[pl.ds(g0, gpb), :])
    row_scr[1:2, :] = side_by_side(aim_ref[pl.ds(g0, gpb), :])
    row_scr[3:4, 0:LANES] = side_by_side(d_ref[pl.ds(g0, gpb), :])
    ldt_groups = ldt_ref[pl.ds(pl.program_id(0), 1), :]
    col_group = lax.broadcasted_iota(jnp.int32, (1, nst), 1) >> 6
    ldt = jnp.zeros((1, nst), F32)
    for gi in range(gpb):
        ldt = jnp.where(col_group == gi, ldt_groups[:, gi:gi + 1], ldt)
    row_scr[2:3, :] = ldt
    are = row_scr[0:1, :]
    aim = row_scr[1:2, :]
    dt = jnp.exp(row_scr[2:3, :])
    d_row = row_scr[3:4, 0:LANES]
    er = jnp.exp(are * dt)
    abr = er * jnp.cos(aim * dt)
    abi = er * jnp.sin(aim * dt)
    den = are * are + aim * aim
    cr = ((abr - 1.0) * are + abi * aim) / den
    ci = (abi * are - (abr - 1.0) * aim) / den
    btr = btr_ref[0]
    bti = bti_ref[0]
    bbr = cr * btr - ci * bti
    bbi = cr * bti + ci * btr
    ctr = jnp.concatenate([cre_ref[g0 + gi] for gi in range(gpb)], axis=1)
    cti = jnp.concatenate([cim_ref[g0 + gi] for gi in range(gpb)], axis=1)

    grow = lax.broadcasted_iota(jnp.int32, (LANES, 2 * nst), 0) >> 4
    gcol = (lax.broadcasted_iota(jnp.int32, (LANES, 2 * nst), 1) & (nst - 1)) >> 6
    same = grow == gcol

    def expand(re, im):
        x = jnp.concatenate([re, im], axis=1)
        x = jnp.concatenate([x] * S5_GROUPS_PER_BLOCK, axis=0)
        return jnp.where(same, x, 0.0).astype(BF16)

    pr = jnp.ones_like(abr)
    pi = jnp.zeros_like(abr)
    for k in range(S5_T + 1):
        if k < S5_T:
            rs = slice((S5_T - 1 - k) * LANES, (S5_T - k) * LANES)
            bx_scr[rs, :] = expand(pr * bbr - pi * bbi, pr * bbi + pi * bbr)
        cyt_scr[k * LANES:(k + 1) * LANES, :] = expand(ctr * pr - cti * pi, -(ctr * pi + cti * pr))
        if k < S5_T:
            pr, pi = pr * abr - pi * abi, pr * abi + pi * abr
    at_r, at_i = pr, pi

    kst_f32 = _dot_nt(bx_scr[...], cyt_scr[0:LANES, :])
    kst = kst_f32.astype(BF16)
    eye = (lax.broadcasted_iota(jnp.int32, (LANES, LANES), 0)
           == lax.broadcasted_iota(jnp.int32, (LANES, LANES), 1))
    k0 = (kst_f32[(S5_T - 1) * LANES:, :] + jnp.where(eye, d_row, 0.0)).astype(BF16)
    k2_scr[:, LANES:2 * LANES] = kst
    k2_scr[(S5_T - 1) * LANES:, LANES:2 * LANES] = k0
    k2_scr[0:(S5_T - 1) * LANES, 0:LANES] = kst[LANES:, :]
    k2_scr[(S5_T - 2) * LANES:(S5_T - 1) * LANES, 0:LANES] = k0
    k2_scr[(S5_T - 1) * LANES:, 0:LANES] = jnp.zeros((LANES, LANES), BF16)

    for t in range(S5_T):
        ucat_scr[:, t * LANES:(t + 1) * LANES] = (
            u_ref[0, pl.ds(t, S5_CHUNKS, stride=S5_T), :].astype(BF16))
    u = ucat_scr[...]

    part = S5_CHUNKS // S5_PARTS
    for c0 in range(0, S5_CHUNKS, part):
        x_scr[c0:c0 + part, :] = _dot(u[c0:c0 + part, :], bx_scr[...])

    sr = jnp.zeros((1, nst), F32)
    si = jnp.zeros((1, nst), F32)
    for c0 in range(0, S5_CHUNKS, part):
        for c in range(c0, c0 + part):
            xr = x_scr[c:c + 1, 0:nst]
            xi = x_scr[c:c + 1, nst:2 * nst]
            x_scr[c:c + 1, 0:nst] = sr
            x_scr[c:c + 1, nst:2 * nst] = si
            sr, si = at_r * sr - at_i * si + xr, at_r * si + at_i * sr + xi
        s_in = x_scr[c0:c0 + part, :].astype(BF16)
        uh = u[c0:c0 + part, :]
        for t in range(0, S5_T, 2):
            y2 = _dot(uh[:, 0:(t + 2) * LANES], k2_scr[(S5_T - 2 - t) * LANES:, :])
            y2 += _dot_nt(s_in, cyt_scr[(t + 1) * LANES:(t + 3) * LANES, :])
            for q in range(2):
                rows = pl.ds(c0 * S5_T + t + q, part, stride=S5_T)
                y_ref[0, rows, :] = y2[:, q * LANES:(q + 1) * LANES]


def _s5_core(u3, b_t, per_group, w_out, glu_w):
    blk3 = lambda *s: pl.BlockSpec((1,) + s, lambda b: (b, 0, 0))
    casts = [w_out, glu_w]
    return pl.pallas_call(
        _s5_kernel,
        grid=(S5_BLOCKS,),
        in_specs=[blk3(SEQ, LANES)] + [blk3(*b.shape[1:]) for b in b_t]
        + [_const_spec(p.shape) for p in per_group]
        + [_slab_spec(w.shape, S5_BLOCKS) for w in casts],
        out_specs=[blk3(SEQ, LANES)] + [_slab_spec(w.shape, S5_BLOCKS) for w in casts],
        out_shape=[jax.ShapeDtypeStruct((S5_BLOCKS, SEQ, LANES), F32)]
        + [jax.ShapeDtypeStruct(w.shape, BF16) for w in casts],
        scratch_shapes=[
            pltpu.VMEM((S5_CHUNKS, S5_CHUNK_COLS), BF16),
            pltpu.VMEM((S5_CHUNK_COLS, 2 * S5_BLOCK_STATE), BF16),
            pltpu.VMEM(((S5_T + 1) * LANES, 2 * S5_BLOCK_STATE), BF16),
            pltpu.VMEM((S5_CHUNK_COLS, 2 * LANES), BF16),
            pltpu.VMEM((S5_CHUNKS, 2 * S5_BLOCK_STATE), F32),
            pltpu.VMEM((8, S5_BLOCK_STATE), F32),
        ],
        compiler_params=pltpu.CompilerParams(
            dimension_semantics=("arbitrary",), vmem_limit_bytes=VMEM_LIMIT),
        name="s5_core",
    )(u3, *b_t, *per_group, *casts)


def _out_proj_kernel(y_ref, sgu_ref, x_ref, gluw_ref, glub_ref, g_ref, wout_ref, gm_ref,
                     wdn_ref, o_ref, h_ref, wdnb_ref):
    def glu(r0):
        rs = slice(r0, r0 + SUB_ROWS)
        y = jnp.concatenate([y_ref[b, rs, :] for b in range(S5_BLOCKS)], axis=1)
        y = jax.nn.gelu(y)
        gate = jax.nn.sigmoid(_dot(y.astype(BF16), gluw_ref[...]) + glub_ref[...])
        return _rms(y * gate, g_ref[...]).astype(BF16)

    starts = list(range(0, x_ref.shape[0], SUB_ROWS))
    gated = [glu(r0) for r0 in starts]
    projs = [_dot(jnp.concatenate([a_n, sgu_ref[r0:r0 + SUB_ROWS, :]], axis=1), wout_ref[...])
             for a_n, r0 in zip(gated, starts)]
    for proj, r0 in zip(projs, starts):
        rs = slice(r0, r0 + SUB_ROWS)
        x1 = x_ref[rs, :] + proj
        o_ref[rs, :] = x1
        h_ref[rs, :] = _rms(x1, gm_ref[...]).astype(BF16)
    wdnb_ref[...] = wdn_ref[...].astype(BF16)


def _out_proj(y3, sgu_n, x, glu_w, glu_b, g, w_out, g_mlp, w_down):
    tm = ROW_TILE
    steps = SEQ // tm
    return pl.pallas_call(
        _out_proj_kernel,
        grid=(steps,),
        in_specs=[
            pl.BlockSpec((S5_BLOCKS, tm, LANES), lambda i: (0, i, 0)),
            pl.BlockSpec((tm, SGU_WIDTH), lambda i: (i, 0)),
            pl.BlockSpec((tm, D_MODEL), lambda i: (i, 0)),
            _const_spec((SSM_WIDTH, SSM_WIDTH)),
            _const_spec((1, SSM_WIDTH)),
            _const_spec((1, SSM_WIDTH)),
            _const_spec((D_MODEL, D_MODEL)),
            _const_spec((1, D_MODEL)),
            _slab_spec(w_down.shape, steps),
        ],
        out_specs=[
            pl.BlockSpec((tm, D_MODEL), lambda i: (i, 0)),
            pl.BlockSpec((tm, D_MODEL), lambda i: (i, 0)),
            _slab_spec(w_down.shape, steps),
        ],
        out_shape=[
            jax.ShapeDtypeStruct((SEQ, D_MODEL), F32),
            jax.ShapeDtypeStruct((SEQ, D_MODEL), BF16),
            jax.ShapeDtypeStruct(w_down.shape, BF16),
        ],
        compiler_params=pltpu.CompilerParams(
            dimension_semantics=("arbitrary",), vmem_limit_bytes=VMEM_LIMIT),
        name="s5_glu_out_proj",
    )(y3, sgu_n, x, glu_w, glu_b, g, w_out, g_mlp, w_down)


def _mlp_kernel(h_ref, x1_ref, wu_ref, wd_ref, gf_ref, o_ref):
    j = pl.program_id(1)
    nj = pl.num_programs(1)
    tm = o_ref.shape[0]
    xr = x1_ref.shape[0]

    def contribution(rs):
        r = jnp.square(jnp.maximum(_dot(h_ref[rs, :], wu_ref[...]), 0.0)).astype(BF16)
        return _dot(r, wd_ref[...])

    @pl.when(j == 0)
    def _():
        o_ref[...] = contribution(slice(None))
        o_ref[0:xr, :] += x1_ref[...]

    @pl.when((j > 0) & (j < nj - 1))
    def _():
        o_ref[...] += contribution(slice(None))
        rows = pl.ds(pl.multiple_of(j * xr, xr), xr)
        o_ref[rows, :] += x1_ref[...]

    @pl.when(j == nj - 1)
    def _():
        o_ref[tm - xr:tm, :] += x1_ref[...]
        starts = list(range(0, tm, SUB_ROWS))
        nxt = contribution(slice(starts[0], starts[0] + SUB_ROWS))
        for k, r0 in enumerate(starts):
            rs = slice(r0, r0 + SUB_ROWS)
            cur = nxt
            if k + 1 < len(starts):
                nxt = contribution(slice(starts[k + 1], starts[k + 1] + SUB_ROWS))
            o_ref[rs, :] = _rms(o_ref[rs, :] + cur, gf_ref[...])


def _mlp(h, x1, w_up, w_down, gf):
    tm, tf = MLP_ROW_TILE, MLP_FF_TILE
    nj = D_FF // tf
    return pl.pallas_call(
        _mlp_kernel,
        grid=(SEQ // tm, nj),
        in_specs=[
            pl.BlockSpec((tm, D_MODEL), lambda i, j: (i, 0)),
            pl.BlockSpec((tm // nj, D_MODEL), lambda i, j: (i * nj + j, 0)),
            pl.BlockSpec((D_MODEL, tf), lambda i, j: (0, j)),
            pl.BlockSpec((tf, D_MODEL), lambda i, j: (j, 0)),
            _const_spec((1, D_MODEL)),
        ],
        out_specs=pl.BlockSpec((tm, D_MODEL), lambda i, j: (i, 0)),
        out_shape=jax.ShapeDtypeStruct((SEQ, D_MODEL), F32),
        compiler_params=pltpu.CompilerParams(
            dimension_semantics=("arbitrary", "arbitrary"), vmem_limit_bytes=VMEM_LIMIT),
        name="mlp_final_norm",
    )(h, x1, w_up, w_down, gf)


def _s5_params(a_re, a_im, b_re, b_im, c_re, c_im, d, log_dt):
    nb, gpb, p, h = S5_BLOCKS, S5_GROUPS_PER_BLOCK, SSM_STATE, SSM_GROUP
    groups = nb * gpb

    def b_layout(b):
        return jnp.transpose(b.reshape(nb, gpb, p, h), (0, 3, 1, 2)).reshape(nb, h, gpb * p)

    per_group = (a_re.reshape(groups, p), a_im.reshape(groups, p), log_dt.reshape(nb, gpb),
                 d.reshape(groups, h), c_re.reshape(groups, h, p), c_im.reshape(groups, h, p))
    return (b_layout(b_re), b_layout(b_im)), per_group


def kernel(x, norm_mix_g, w_in, ssm_a_re, ssm_a_im, ssm_b_re, ssm_b_im, ssm_c_re, ssm_c_im,
           ssm_d, ssm_log_dt, ssm_glu_w, ssm_glu_b, sgu_ln_g, sgu_ln_b, sgu_w, sgu_b,
           out_norm_ssm_g, out_norm_sgu_g, w_out, norm_mlp_g, w_up, w_down, norm_final_g):
    assert norm_mix_g.shape[0] == 1, "single-layer problem: the MLP call applies the final norm"
    xs = x.reshape(SEQ, D_MODEL)
    row = lambda v: v.reshape(1, -1)
    mat = lambda w: w.reshape(w.shape[-2:])
    u3, sgu_n, w_up_bf = _in_proj(
        xs, row(norm_mix_g), mat(w_in), row(sgu_ln_g), row(sgu_ln_b),
        sgu_w.reshape(SGU_HEADS, SGU_CHUNK, SGU_CHUNK), mat(sgu_b), row(out_norm_sgu_g), mat(w_up))
    b_t, per_group = _s5_params(ssm_a_re, ssm_a_im, ssm_b_re, ssm_b_im, ssm_c_re, ssm_c_im,
                                ssm_d, ssm_log_dt)
    y3, w_out_bf, glu_w_bf = _s5_core(u3, b_t, per_group, mat(w_out), mat(ssm_glu_w))
    x1, h, w_down_bf = _out_proj(y3, sgu_n, xs, glu_w_bf, row(ssm_glu_b), row(out_norm_ssm_g),
                                 w_out_bf, row(norm_mlp_g), mat(w_down))
    out = _mlp(h, x1, w_up_bf, w_down_bf, row(norm_final_g))
    return out.reshape(x.shape)
```

```python
import jax
import jax.numpy as jnp
from jax import lax
from jax.experimental import pallas as pl
from jax.experimental.pallas import tpu as pltpu

D_MODEL = 2048
SEQ = 8192
SSM_WIDTH = 1024
SSM_GROUP = 16
SSM_STATE = 64
SGU_WIDTH = 1024
SGU_HEADS = 8
SGU_HEAD_DIM = 128
SGU_CHUNK = 128
IN_WIDTH = SSM_WIDTH + 2 * SGU_WIDTH
D_FF = 4 * D_MODEL
EPS = 1e-6

LANES = 128
S5_T = 8
S5_CHUNKS = SEQ // S5_T
S5_BLOCKS = SSM_WIDTH // LANES
S5_GROUPS_PER_BLOCK = LANES // SSM_GROUP
S5_BLOCK_STATE = S5_GROUPS_PER_BLOCK * SSM_STATE
S5_CHUNK_COLS = S5_T * LANES
S5_PARTS = 4

ROW_TILE = 512
SUB_ROWS = 256
W_IN_STAGE_ROWS = 128
W_IN_STAGE_SLOTS = 4
PIPE_DEPTH = 2
MLP_ROW_TILE = 1024
MLP_FF_TILE = 1024
V7X_VMEM_BYTES = 64 * 1024 * 1024
VMEM_LIMIT = V7X_VMEM_BYTES - 4 * 1024 * 1024

F32 = jnp.float32
BF16 = jnp.bfloat16


def _rms(x, g):
    return x * lax.rsqrt(jnp.mean(x * x, axis=-1, keepdims=True) + EPS) * g


def _const_spec(shape):
    n = len(shape)
    return pl.BlockSpec(shape, lambda *_: (0,) * n, pipeline_mode=pl.Buffered(1))


def _dot(a, b):
    return jnp.dot(a, b, preferred_element_type=F32)


def _dot_nt(a, b):
    return lax.dot_general(a, b, (((1,), (1,)), ((), ())), preferred_element_type=F32)


def _in_proj_kernel(x_ref, g_ref, win_hbm, lng_ref, lnb_ref, sw_ref, sb_ref,
                    og_ref, wup_ref, u_ref, sgu_ref, wupb_ref, y_scr, win_ref, stage, sem):
    tm = x_ref.shape[0]

    first_step = pl.program_id(0) == 0
    slots, rows = stage.shape[0], stage.shape[1]
    chunks = win_hbm.shape[0] // rows

    def chunk_copy(k):
        return pltpu.make_async_copy(
            win_hbm.at[pl.ds(k * rows, rows), :], stage.at[k % slots], sem.at[k % slots])

    @pl.when(first_step)
    def _():
        for k in range(slots - 1):
            chunk_copy(k).start()

    wupb_ref[...] = wup_ref[...].astype(BF16)

    @pl.when(first_step)
    def _():
        for k in range(chunks):
            if k + slots - 1 < chunks:
                chunk_copy(k + slots - 1).start()
            chunk_copy(k).wait()
            win_ref[k * rows:(k + 1) * rows, :] = stage[k % slots].astype(BF16)

    row = lax.broadcasted_iota(jnp.int32, (SGU_CHUNK, SGU_CHUNK), 0)
    col = lax.broadcasted_iota(jnp.int32, (SGU_CHUNK, SGU_CHUNK), 1)
    causal = row >= col
    ws = [jnp.where(causal, sw_ref[hd], 0.0).astype(BF16) for hd in range(SGU_HEADS)]
    sb_t = sb_ref[...].T

    def project(r0):
        h = _rms(x_ref[r0:r0 + SUB_ROWS, :], g_ref[...]).astype(BF16)
        zv = _dot(h, win_ref[:, SSM_WIDTH + SGU_WIDTH:])
        zu = _dot(h, win_ref[:, SSM_WIDTH:SSM_WIDTH + SGU_WIDTH])
        zs = _dot(h, win_ref[:, :SSM_WIDTH])
        return zs, zu, zv

    def mix(r0, z):
        zs, zu, zv = z
        for b in range(S5_BLOCKS):
            u_ref[b, r0:r0 + SUB_ROWS, :] = zs[:, b * LANES:(b + 1) * LANES]

        v = jax.nn.gelu(zv)
        u = jax.nn.gelu(zu)
        mu = jnp.mean(v, axis=-1, keepdims=True)
        vc = v - mu
        var = jnp.mean(vc * vc, axis=-1, keepdims=True)
        vb = (vc * lax.rsqrt(var + EPS) * lng_ref[...] + lnb_ref[...]).astype(BF16)
        for hd in range(SGU_HEADS):
            cs = slice(hd * SGU_HEAD_DIM, (hd + 1) * SGU_HEAD_DIM)
            for ck in range(SUB_ROWS // SGU_CHUNK):
                rs = slice(ck * SGU_CHUNK, (ck + 1) * SGU_CHUNK)
                ys = slice(r0 + ck * SGU_CHUNK, r0 + (ck + 1) * SGU_CHUNK)
                y_scr[ys, cs] = u[rs, cs] * (_dot(ws[hd], vb[rs, cs]) + sb_t[:, hd:hd + 1])
        sgu_ref[r0:r0 + SUB_ROWS, :] = _rms(y_scr[r0:r0 + SUB_ROWS, :], og_ref[...]).astype(BF16)

    starts = list(range(0, tm, SUB_ROWS))
    ahead = [project(r0) for r0 in starts[:PIPE_DEPTH]]
    for k, r0 in enumerate(starts):
        z = ahead.pop(0)
        if k + PIPE_DEPTH < len(starts):
            ahead.append(project(starts[k + PIPE_DEPTH]))
        mix(r0, z)


def _slab_spec(shape, steps):
    return pl.BlockSpec((shape[0] // steps, shape[1]), lambda i: (i, 0))


def _in_proj(x, g, w_in, ln_g, ln_b, sgu_w, sgu_b, og, w_up):
    tm = ROW_TILE
    steps = SEQ // tm
    casts = [w_up]
    return pl.pallas_call(
        _in_proj_kernel,
        grid=(steps,),
        in_specs=[
            pl.BlockSpec((tm, D_MODEL), lambda i: (i, 0)),
            _const_spec((1, D_MODEL)),
            pl.BlockSpec(memory_space=pl.ANY),
            _const_spec((1, SGU_WIDTH)),
            _const_spec((1, SGU_WIDTH)),
            _const_spec((SGU_HEADS, SGU_CHUNK, SGU_CHUNK)),
            _const_spec((SGU_HEADS, SGU_CHUNK)),
            _const_spec((1, SGU_WIDTH)),
        ] + [_slab_spec(w.shape, steps) for w in casts],
        out_specs=[
            pl.BlockSpec((S5_BLOCKS, tm, LANES), lambda i: (0, i, 0)),
            pl.BlockSpec((tm, SGU_WIDTH), lambda i: (i, 0)),
        ] + [_slab_spec(w.shape, steps) for w in casts],
        out_shape=[
            jax.ShapeDtypeStruct((S5_BLOCKS, SEQ, LANES), F32),
            jax.ShapeDtypeStruct((SEQ, SGU_WIDTH), BF16),
        ] + [jax.ShapeDtypeStruct(w.shape, BF16) for w in casts],
        scratch_shapes=[
            pltpu.VMEM((tm, SGU_WIDTH), F32),
            pltpu.VMEM((D_MODEL, IN_WIDTH), BF16),
            pltpu.VMEM((W_IN_STAGE_SLOTS, W_IN_STAGE_ROWS, IN_WIDTH), F32),
            pltpu.SemaphoreType.DMA((W_IN_STAGE_SLOTS,)),
        ],
        compiler_params=pltpu.CompilerParams(
            dimension_semantics=("arbitrary",), vmem_limit_bytes=VMEM_LIMIT),
        name="in_proj_sgu",
    )(x, g, w_in, ln_g, ln_b, sgu_w, sgu_b, og, *casts)


def _s5_kernel(u_ref, btr_ref, bti_ref, are_ref, aim_ref, ldt_ref, d_ref, cre_ref, cim_ref,
               wout_ref, gluw_ref,
               y_ref, woutb_ref, gluwb_ref, ucat_scr, bx_scr, cyt_scr, k2_scr, x_scr, row_scr):
    nst = S5_BLOCK_STATE
    gpb = S5_GROUPS_PER_BLOCK

    def side_by_side(rows):
        return jnp.concatenate([rows[gi:gi + 1, :] for gi in range(gpb)], axis=1)

    woutb_ref[...] = wout_ref[...].astype(BF16)
    gluwb_ref[...] = gluw_ref[...].astype(BF16)

    row_scr[0:1, :] = side_by_side(are_ref[0])
    row_scr[1:2, :] = side_by_side(aim_ref[0])
    row_scr[3:4, 0:LANES] = side_by_side(d_ref[0])
    ldt_groups = ldt_ref[0]
    col_group = lax.broadcasted_iota(jnp.int32, (1, nst), 1) >> 6
    ldt = jnp.zeros((1, nst), F32)
    for gi in range(gpb):
        ldt = jnp.where(col_group == gi, ldt_groups[:, gi:gi + 1], ldt)
    row_scr[2:3, :] = ldt
    are = row_scr[0:1, :]
    aim = row_scr[1:2, :]
    dt = jnp.exp(row_scr[2:3, :])
    d_row = row_scr[3:4, 0:LANES]
    er = jnp.exp(are * dt)
    abr = er * jnp.cos(aim * dt)
    abi = er * jnp.sin(aim * dt)
    den = are * are + aim * aim
    cr = ((abr - 1.0) * are + abi * aim) / den
    ci = (abi * are - (abr - 1.0) * aim) / den
    btr = btr_ref[0]
    bti = bti_ref[0]
    bbr = cr * btr - ci * bti
    bbi = cr * bti + ci * btr
    ctr = jnp.concatenate([cre_ref[0, gi] for gi in range(gpb)], axis=1)
    cti = jnp.concatenate([cim_ref[0, gi] for gi in range(gpb)], axis=1)

    grow = lax.broadcasted_iota(jnp.int32, (LANES, 2 * nst), 0) >> 4
    gcol = (lax.broadcasted_iota(jnp.int32, (LANES, 2 * nst), 1) & (nst - 1)) >> 6
    same = grow == gcol

    def expand(re, im):
        x = jnp.concatenate([re, im], axis=1)
        x = jnp.concatenate([x] * S5_GROUPS_PER_BLOCK, axis=0)
        return jnp.where(same, x, 0.0).astype(BF16)

    pr = jnp.ones_like(abr)
    pi = jnp.zeros_like(abr)
    for k in range(S5_T + 1):
        if k < S5_T:
            rs = slice((S5_T - 1 - k) * LANES, (S5_T - k) * LANES)
            bx_scr[rs, :] = expand(pr * bbr - pi * bbi, pr * bbi + pi * bbr)
        cyt_scr[k * LANES:(k + 1) * LANES, :] = expand(ctr * pr - cti * pi, -(ctr * pi + cti * pr))
        if k < S5_T:
            pr, pi = pr * abr - pi * abi, pr * abi + pi * abr
    at_r, at_i = pr, pi

    kst_f32 = _dot_nt(bx_scr[...], cyt_scr[0:LANES, :])
    kst = kst_f32.astype(BF16)
    eye = (lax.broadcasted_iota(jnp.int32, (LANES, LANES), 0)
           == lax.broadcasted_iota(jnp.int32, (LANES, LANES), 1))
    k0 = (kst_f32[(S5_T - 1) * LANES:, :] + jnp.where(eye, d_row, 0.0)).astype(BF16)
    k2_scr[:, LANES:2 * LANES] = kst
    k2_scr[(S5_T - 1) * LANES:, LANES:2 * LANES] = k0
    k2_scr[0:(S5_T - 1) * LANES, 0:LANES] = kst[LANES:, :]
    k2_scr[(S5_T - 2) * LANES:(S5_T - 1) * LANES, 0:LANES] = k0
    k2_scr[(S5_T - 1) * LANES:, 0:LANES] = jnp.zeros((LANES, LANES), BF16)

    for t in range(S5_T):
        ucat_scr[:, t * LANES:(t + 1) * LANES] = (
            u_ref[0, pl.ds(t, S5_CHUNKS, stride=S5_T), :].astype(BF16))
    u = ucat_scr[...]

    part = S5_CHUNKS // S5_PARTS
    for c0 in range(0, S5_CHUNKS, part):
        x_scr[c0:c0 + part, :] = _dot(u[c0:c0 + part, :], bx_scr[...])

    sr = jnp.zeros((1, nst), F32)
    si = jnp.zeros((1, nst), F32)
    for c0 in range(0, S5_CHUNKS, part):
        for c in range(c0, c0 + part):
            xr = x_scr[c:c + 1, 0:nst]
            xi = x_scr[c:c + 1, nst:2 * nst]
            x_scr[c:c + 1, 0:nst] = sr
            x_scr[c:c + 1, nst:2 * nst] = si
            sr, si = at_r * sr - at_i * si + xr, at_r * si + at_i * sr + xi
        s_in = x_scr[c0:c0 + part, :].astype(BF16)
        uh = u[c0:c0 + part, :]
        for t in range(0, S5_T, 2):
            y2 = _dot(uh[:, 0:(t + 2) * LANES], k2_scr[(S5_T - 2 - t) * LANES:, :])
            y2 += _dot_nt(s_in, cyt_scr[(t + 1) * LANES:(t + 3) * LANES, :])
            for q in range(2):
                rows = pl.ds(c0 * S5_T + t + q, part, stride=S5_T)
                y_ref[0, rows, :] = y2[:, q * LANES:(q + 1) * LANES]


def _s5_core(u3, b_t, per_group, w_out, glu_w):
    blk3 = lambda *s: pl.BlockSpec((1,) + s, lambda b: (b, 0, 0))
    per_block = lambda a: pl.BlockSpec((1,) + a.shape[1:], lambda b: (b,) + (0,) * (a.ndim - 1))
    casts = [w_out, glu_w]
    return pl.pallas_call(
        _s5_kernel,
        grid=(S5_BLOCKS,),
        in_specs=[blk3(SEQ, LANES)] + [blk3(*b.shape[1:]) for b in b_t]
        + [per_block(p) for p in per_group]
        + [_slab_spec(w.shape, S5_BLOCKS) for w in casts],
        out_specs=[blk3(SEQ, LANES)] + [_slab_spec(w.shape, S5_BLOCKS) for w in casts],
        out_shape=[jax.ShapeDtypeStruct((S5_BLOCKS, SEQ, LANES), F32)]
        + [jax.ShapeDtypeStruct(w.shape, BF16) for w in casts],
        scratch_shapes=[
            pltpu.VMEM((S5_CHUNKS, S5_CHUNK_COLS), BF16),
            pltpu.VMEM((S5_CHUNK_COLS, 2 * S5_BLOCK_STATE), BF16),
            pltpu.VMEM(((S5_T + 1) * LANES, 2 * S5_BLOCK_STATE), BF16),
            pltpu.VMEM((S5_CHUNK_COLS, 2 * LANES), BF16),
            pltpu.VMEM((S5_CHUNKS, 2 * S5_BLOCK_STATE), F32),
            pltpu.VMEM((8, S5_BLOCK_STATE), F32),
        ],
        compiler_params=pltpu.CompilerParams(
            dimension_semantics=("arbitrary",), vmem_limit_bytes=VMEM_LIMIT),
        name="s5_core",
    )(u3, *b_t, *per_group, *casts)


def _out_proj_kernel(y_ref, sgu_ref, x_ref, gluw_ref, glub_ref, g_ref, wout_ref, gm_ref,
                     wdn_ref, o_ref, h_ref, wdnb_ref):
    def glu(r0):
        rs = slice(r0, r0 + SUB_ROWS)
        y = jnp.concatenate([y_ref[b, rs, :] for b in range(S5_BLOCKS)], axis=1)
        y = jax.nn.gelu(y)
        gate = jax.nn.sigmoid(_dot(y.astype(BF16), gluw_ref[...]) + glub_ref[...])
        return _rms(y * gate, g_ref[...]).astype(BF16)

    starts = list(range(0, x_ref.shape[0], SUB_ROWS))
    gated = [glu(r0) for r0 in starts]
    projs = [_dot(jnp.concatenate([a_n, sgu_ref[r0:r0 + SUB_ROWS, :]], axis=1), wout_ref[...])
             for a_n, r0 in zip(gated, starts)]
    for proj, r0 in zip(projs, starts):
        rs = slice(r0, r0 + SUB_ROWS)
        x1 = x_ref[rs, :] + proj
        o_ref[rs, :] = x1
        h_ref[rs, :] = _rms(x1, gm_ref[...]).astype(BF16)
    wdnb_ref[...] = wdn_ref[...].astype(BF16)


def _out_proj(y3, sgu_n, x, glu_w, glu_b, g, w_out, g_mlp, w_down):
    tm = ROW_TILE
    steps = SEQ // tm
    return pl.pallas_call(
        _out_proj_kernel,
        grid=(steps,),
        in_specs=[
            pl.BlockSpec((S5_BLOCKS, tm, LANES), lambda i: (0, i, 0)),
            pl.BlockSpec((tm, SGU_WIDTH), lambda i: (i, 0)),
            pl.BlockSpec((tm, D_MODEL), lambda i: (i, 0)),
            _const_spec((SSM_WIDTH, SSM_WIDTH)),
            _const_spec((1, SSM_WIDTH)),
            _const_spec((1, SSM_WIDTH)),
            _const_spec((D_MODEL, D_MODEL)),
            _const_spec((1, D_MODEL)),
            _slab_spec(w_down.shape, steps),
        ],
        out_specs=[
            pl.BlockSpec((tm, D_MODEL), lambda i: (i, 0)),
            pl.BlockSpec((tm, D_MODEL), lambda i: (i, 0)),
            _slab_spec(w_down.shape, steps),
        ],
        out_shape=[
            jax.ShapeDtypeStruct((SEQ, D_MODEL), F32),
            jax.ShapeDtypeStruct((SEQ, D_MODEL), BF16),
            jax.ShapeDtypeStruct(w_down.shape, BF16),
        ],
        compiler_params=pltpu.CompilerParams(
            dimension_semantics=("arbitrary",), vmem_limit_bytes=VMEM_LIMIT),
        name="s5_glu_out_proj",
    )(y3, sgu_n, x, glu_w, glu_b, g, w_out, g_mlp, w_down)


def _mlp_kernel(h_ref, x1_ref, wu_ref, wd_ref, gf_ref, o_ref):
    j = pl.program_id(1)
    nj = pl.num_programs(1)
    tm = o_ref.shape[0]
    xr = x1_ref.shape[0]

    def contribution(rs):
        r = jnp.square(jnp.maximum(_dot(h_ref[rs, :], wu_ref[...]), 0.0)).astype(BF16)
        return _dot(r, wd_ref[...])

    @pl.when(j == 0)
    def _():
        o_ref[...] = contribution(slice(None))
        o_ref[0:xr, :] += x1_ref[...]

    @pl.when((j > 0) & (j < nj - 1))
    def _():
        o_ref[...] += contribution(slice(None))
        rows = pl.ds(pl.multiple_of(j * xr, xr), xr)
        o_ref[rows, :] += x1_ref[...]

    @pl.when(j == nj - 1)
    def _():
        o_ref[tm - xr:tm, :] += x1_ref[...]
        starts = list(range(0, tm, SUB_ROWS))
        nxt = contribution(slice(starts[0], starts[0] + SUB_ROWS))
        for k, r0 in enumerate(starts):
            rs = slice(r0, r0 + SUB_ROWS)
            cur = nxt
            if k + 1 < len(starts):
                nxt = contribution(slice(starts[k + 1], starts[k + 1] + SUB_ROWS))
            o_ref[rs, :] = _rms(o_ref[rs, :] + cur, gf_ref[...])


def _mlp(h, x1, w_up, w_down, gf):
    tm, tf = MLP_ROW_TILE, MLP_FF_TILE
    nj = D_FF // tf
    return pl.pallas_call(
        _mlp_kernel,
        grid=(SEQ // tm, nj),
        in_specs=[
            pl.BlockSpec((tm, D_MODEL), lambda i, j: (i, 0)),
            pl.BlockSpec((tm // nj, D_MODEL), lambda i, j: (i * nj + j, 0)),
            pl.BlockSpec((D_MODEL, tf), lambda i, j: (0, j)),
            pl.BlockSpec((tf, D_MODEL), lambda i, j: (j, 0)),
            _const_spec((1, D_MODEL)),
        ],
        out_specs=pl.BlockSpec((tm, D_MODEL), lambda i, j: (i, 0)),
        out_shape=jax.ShapeDtypeStruct((SEQ, D_MODEL), F32),
        compiler_params=pltpu.CompilerParams(
            dimension_semantics=("arbitrary", "arbitrary"), vmem_limit_bytes=VMEM_LIMIT),
        name="mlp_final_norm",
    )(h, x1, w_up, w_down, gf)


def _s5_params(a_re, a_im, b_re, b_im, c_re, c_im, d, log_dt):
    nb, gpb, p, h = S5_BLOCKS, S5_GROUPS_PER_BLOCK, SSM_STATE, SSM_GROUP

    def b_layout(b):
        return jnp.transpose(b.reshape(nb, gpb, p, h), (0, 3, 1, 2)).reshape(nb, h, gpb * p)

    per_group = (a_re.reshape(nb, gpb, p), a_im.reshape(nb, gpb, p), log_dt.reshape(nb, 1, gpb),
                 d.reshape(nb, gpb, h), c_re.reshape(nb, gpb, h, p), c_im.reshape(nb, gpb, h, p))
    return (b_layout(b_re), b_layout(b_im)), per_group


def kernel(x, norm_mix_g, w_in, ssm_a_re, ssm_a_im, ssm_b_re, ssm_b_im, ssm_c_re, ssm_c_im,
           ssm_d, ssm_log_dt, ssm_glu_w, ssm_glu_b, sgu_ln_g, sgu_ln_b, sgu_w, sgu_b,
           out_norm_ssm_g, out_norm_sgu_g, w_out, norm_mlp_g, w_up, w_down, norm_final_g):
    assert norm_mix_g.shape[0] == 1, "single-layer problem: the MLP call applies the final norm"
    xs = x.reshape(SEQ, D_MODEL)
    row = lambda v: v.reshape(1, -1)
    mat = lambda w: w.reshape(w.shape[-2:])
    u3, sgu_n, w_up_bf = _in_proj(
        xs, row(norm_mix_g), mat(w_in), row(sgu_ln_g), row(sgu_ln_b),
        sgu_w.reshape(SGU_HEADS, SGU_CHUNK, SGU_CHUNK), mat(sgu_b), row(out_norm_sgu_g), mat(w_up))
    b_t, per_group = _s5_params(ssm_a_re, ssm_a_im, ssm_b_re, ssm_b_im, ssm_c_re, ssm_c_im,
                                ssm_d, ssm_log_dt)
    y3, w_out_bf, glu_w_bf = _s5_core(u3, b_t, per_group, mat(w_out), mat(ssm_glu_w))
    x1, h, w_down_bf = _out_proj(y3, sgu_n, xs, glu_w_bf, row(ssm_glu_b), row(out_norm_ssm_g),
                                 w_out_bf, row(norm_mlp_g), mat(w_down))
    out = _mlp(h, x1, w_up_bf, w_down_bf, row(norm_final_g))
    return out.reshape(x.shape)
```

```python
import jax
import jax.numpy as jnp
from jax import lax
from jax.experimental import pallas as pl
from jax.experimental.pallas import tpu as pltpu

D_MODEL = 2048
SEQ = 8192
SSM_WIDTH = 1024
SSM_GROUP = 16
SSM_STATE = 64
SGU_WIDTH = 1024
SGU_HEADS = 8
SGU_HEAD_DIM = 128
SGU_CHUNK = 128
IN_WIDTH = SSM_WIDTH + 2 * SGU_WIDTH
D_FF = 4 * D_MODEL
EPS = 1e-6

LANES = 128
S5_T = 8
S5_CHUNKS = SEQ // S5_T
S5_BLOCKS = SSM_WIDTH // LANES
S5_GROUPS_PER_BLOCK = LANES // SSM_GROUP
S5_BLOCK_STATE = S5_GROUPS_PER_BLOCK * SSM_STATE
S5_CHUNK_COLS = S5_T * LANES
S5_PARTS = 4

ROW_TILE = 512
SUB_ROWS = 256
W_IN_STAGE_ROWS = 128
W_IN_STAGE_SLOTS = 4
PIPE_DEPTH = 2
MLP_ROW_TILE = 1024
MLP_FF_TILE = 1024
V7X_VMEM_BYTES = 64 * 1024 * 1024
VMEM_LIMIT = V7X_VMEM_BYTES - 4 * 1024 * 1024

F32 = jnp.float32
BF16 = jnp.bfloat16


def _rms(x, g):
    return x * lax.rsqrt(jnp.mean(x * x, axis=-1, keepdims=True) + EPS) * g


def _const_spec(shape):
    n = len(shape)
    return pl.BlockSpec(shape, lambda *_: (0,) * n, pipeline_mode=pl.Buffered(1))


def _dot(a, b):
    return jnp.dot(a, b, preferred_element_type=F32)


def _dot_nt(a, b):
    return lax.dot_general(a, b, (((1,), (1,)), ((), ())), preferred_element_type=F32)


def _in_proj_kernel(x_ref, g_ref, win_hbm, lng_ref, lnb_ref, sw_ref, sb_ref,
                    og_ref, wup_ref, u_ref, sgu_ref, wupb_ref, y_scr, win_ref, stage, sem):
    tm = x_ref.shape[0]

    first_step = pl.program_id(0) == 0
    slots, rows = stage.shape[0], stage.shape[1]
    chunks = win_hbm.shape[0] // rows

    def chunk_copy(k):
        return pltpu.make_async_copy(
            win_hbm.at[pl.ds(k * rows, rows), :], stage.at[k % slots], sem.at[k % slots])

    @pl.when(first_step)
    def _():
        for k in range(slots - 1):
            chunk_copy(k).start()

    wupb_ref[...] = wup_ref[...].astype(BF16)

    @pl.when(first_step)
    def _():
        for k in range(chunks):
            if k + slots - 1 < chunks:
                chunk_copy(k + slots - 1).start()
            chunk_copy(k).wait()
            win_ref[k * rows:(k + 1) * rows, :] = stage[k % slots].astype(BF16)

    row = lax.broadcasted_iota(jnp.int32, (SGU_CHUNK, SGU_CHUNK), 0)
    col = lax.broadcasted_iota(jnp.int32, (SGU_CHUNK, SGU_CHUNK), 1)
    causal = row >= col
    ws = [jnp.where(causal, sw_ref[hd], 0.0).astype(BF16) for hd in range(SGU_HEADS)]
    sb_t = sb_ref[...].T

    def project(r0):
        h = _rms(x_ref[r0:r0 + SUB_ROWS, :], g_ref[...]).astype(BF16)
        zv = _dot(h, win_ref[:, SSM_WIDTH + SGU_WIDTH:])
        zu = _dot(h, win_ref[:, SSM_WIDTH:SSM_WIDTH + SGU_WIDTH])
        zs = _dot(h, win_ref[:, :SSM_WIDTH])
        return zs, zu, zv

    def mix(r0, z):
        zs, zu, zv = z
        for b in range(S5_BLOCKS):
            u_ref[b, r0:r0 + SUB_ROWS, :] = zs[:, b * LANES:(b + 1) * LANES]

        v = jax.nn.gelu(zv)
        u = jax.nn.gelu(zu)
        mu = jnp.mean(v, axis=-1, keepdims=True)
        vc = v - mu
        var = jnp.mean(vc * vc, axis=-1, keepdims=True)
        vb = (vc * lax.rsqrt(var + EPS) * lng_ref[...] + lnb_ref[...]).astype(BF16)
        for hd in range(SGU_HEADS):
            cs = slice(hd * SGU_HEAD_DIM, (hd + 1) * SGU_HEAD_DIM)
            for ck in range(SUB_ROWS // SGU_CHUNK):
                rs = slice(ck * SGU_CHUNK, (ck + 1) * SGU_CHUNK)
                ys = slice(r0 + ck * SGU_CHUNK, r0 + (ck + 1) * SGU_CHUNK)
                y_scr[ys, cs] = u[rs, cs] * (_dot(ws[hd], vb[rs, cs]) + sb_t[:, hd:hd + 1])
        sgu_ref[r0:r0 + SUB_ROWS, :] = _rms(y_scr[r0:r0 + SUB_ROWS, :], og_ref[...]).astype(BF16)

    starts = list(range(0, tm, SUB_ROWS))
    ahead = [project(r0) for r0 in starts[:PIPE_DEPTH]]
    for k, r0 in enumerate(starts):
        z = ahead.pop(0)
        if k + PIPE_DEPTH < len(starts):
            ahead.append(project(starts[k + PIPE_DEPTH]))
        mix(r0, z)


def _slab_spec(shape, steps):
    return pl.BlockSpec((shape[0] // steps, shape[1]), lambda i: (i, 0))


def _in_proj(x, g, w_in, ln_g, ln_b, sgu_w, sgu_b, og, w_up):
    tm = ROW_TILE
    steps = SEQ // tm
    casts = [w_up]
    return pl.pallas_call(
        _in_proj_kernel,
        grid=(steps,),
        in_specs=[
            pl.BlockSpec((tm, D_MODEL), lambda i: (i, 0)),
            _const_spec((1, D_MODEL)),
            pl.BlockSpec(memory_space=pl.ANY),
            _const_spec((1, SGU_WIDTH)),
            _const_spec((1, SGU_WIDTH)),
            _const_spec((SGU_HEADS, SGU_CHUNK, SGU_CHUNK)),
            _const_spec((SGU_HEADS, SGU_CHUNK)),
            _const_spec((1, SGU_WIDTH)),
        ] + [_slab_spec(w.shape, steps) for w in casts],
        out_specs=[
            pl.BlockSpec((S5_BLOCKS, tm, LANES), lambda i: (0, i, 0)),
            pl.BlockSpec((tm, SGU_WIDTH), lambda i: (i, 0)),
        ] + [_slab_spec(w.shape, steps) for w in casts],
        out_shape=[
            jax.ShapeDtypeStruct((S5_BLOCKS, SEQ, LANES), F32),
            jax.ShapeDtypeStruct((SEQ, SGU_WIDTH), BF16),
        ] + [jax.ShapeDtypeStruct(w.shape, BF16) for w in casts],
        scratch_shapes=[
            pltpu.VMEM((tm, SGU_WIDTH), F32),
            pltpu.VMEM((D_MODEL, IN_WIDTH), BF16),
            pltpu.VMEM((W_IN_STAGE_SLOTS, W_IN_STAGE_ROWS, IN_WIDTH), F32),
            pltpu.SemaphoreType.DMA((W_IN_STAGE_SLOTS,)),
        ],
        compiler_params=pltpu.CompilerParams(
            dimension_semantics=("arbitrary",), vmem_limit_bytes=VMEM_LIMIT),
        name="in_proj_sgu",
    )(x, g, w_in, ln_g, ln_b, sgu_w, sgu_b, og, *casts)


def _s5_kernel(u_ref, btr_ref, bti_ref, are_ref, aim_ref, ldt_ref, d_ref, cre_ref, cim_ref,
               wout_ref, gluw_ref,
               y_ref, woutb_ref, gluwb_ref, ucat_scr, bx_scr, cyt_scr, k2_scr, x_scr, row_scr):
    nst = S5_BLOCK_STATE
    gpb = S5_GROUPS_PER_BLOCK

    def side_by_side(rows):
        return jnp.concatenate([rows[gi:gi + 1, :] for gi in range(gpb)], axis=1)

    woutb_ref[...] = wout_ref[...].astype(BF16)
    gluwb_ref[...] = gluw_ref[...].astype(BF16)

    row_scr[0:1, :] = side_by_side(are_ref[0])
    row_scr[1:2, :] = side_by_side(aim_ref[0])
    row_scr[3:4, 0:LANES] = side_by_side(d_ref[0])
    ldt_groups = ldt_ref[0]
    col_group = lax.broadcasted_iota(jnp.int32, (1, nst), 1) >> 6
    ldt = jnp.zeros((1, nst), F32)
    for gi in range(gpb):
        ldt = jnp.where(col_group == gi, ldt_groups[:, gi:gi + 1], ldt)
    row_scr[2:3, :] = ldt
    are = row_scr[0:1, :]
    aim = row_scr[1:2, :]
    dt = jnp.exp(row_scr[2:3, :])
    d_row = row_scr[3:4, 0:LANES]
    er = jnp.exp(are * dt)
    abr = er * jnp.cos(aim * dt)
    abi = er * jnp.sin(aim * dt)
    den = are * are + aim * aim
    cr = ((abr - 1.0) * are + abi * aim) / den
    ci = (abi * are - (abr - 1.0) * aim) / den
    btr = jnp.concatenate([btr_ref[0, gi] for gi in range(gpb)], axis=1)
    bti = jnp.concatenate([bti_ref[0, gi] for gi in range(gpb)], axis=1)
    bbr = cr * btr - ci * bti
    bbi = cr * bti + ci * btr
    ctr = jnp.concatenate([cre_ref[0, gi] for gi in range(gpb)], axis=1)
    cti = jnp.concatenate([cim_ref[0, gi] for gi in range(gpb)], axis=1)

    grow = lax.broadcasted_iota(jnp.int32, (LANES, 2 * nst), 0) >> 4
    gcol = (lax.broadcasted_iota(jnp.int32, (LANES, 2 * nst), 1) & (nst - 1)) >> 6
    same = grow == gcol

    def expand(re, im):
        x = jnp.concatenate([re, im], axis=1)
        x = jnp.concatenate([x] * S5_GROUPS_PER_BLOCK, axis=0)
        return jnp.where(same, x, 0.0).astype(BF16)

    pr = jnp.ones_like(abr)
    pi = jnp.zeros_like(abr)
    for k in range(S5_T + 1):
        if k < S5_T:
            rs = slice((S5_T - 1 - k) * LANES, (S5_T - k) * LANES)
            bx_scr[rs, :] = expand(pr * bbr - pi * bbi, pr * bbi + pi * bbr)
        cyt_scr[k * LANES:(k + 1) * LANES, :] = expand(ctr * pr - cti * pi, -(ctr * pi + cti * pr))
        if k < S5_T:
            pr, pi = pr * abr - pi * abi, pr * abi + pi * abr
    at_r, at_i = pr, pi

    kst_f32 = _dot_nt(bx_scr[...], cyt_scr[0:LANES, :])
    kst = kst_f32.astype(BF16)
    eye = (lax.broadcasted_iota(jnp.int32, (LANES, LANES), 0)
           == lax.broadcasted_iota(jnp.int32, (LANES, LANES), 1))
    k0 = (kst_f32[(S5_T - 1) * LANES:, :] + jnp.where(eye, d_row, 0.0)).astype(BF16)
    k2_scr[:, LANES:2 * LANES] = kst
    k2_scr[(S5_T - 1) * LANES:, LANES:2 * LANES] = k0
    k2_scr[0:(S5_T - 1) * LANES, 0:LANES] = kst[LANES:, :]
    k2_scr[(S5_T - 2) * LANES:(S5_T - 1) * LANES, 0:LANES] = k0
    k2_scr[(S5_T - 1) * LANES:, 0:LANES] = jnp.zeros((LANES, LANES), BF16)

    for t in range(S5_T):
        ucat_scr[:, t * LANES:(t + 1) * LANES] = (
            u_ref[0, pl.ds(t, S5_CHUNKS, stride=S5_T), :].astype(BF16))
    u = ucat_scr[...]

    part = S5_CHUNKS // S5_PARTS
    for c0 in range(0, S5_CHUNKS, part):
        x_scr[c0:c0 + part, :] = _dot(u[c0:c0 + part, :], bx_scr[...])

    sr = jnp.zeros((1, nst), F32)
    si = jnp.zeros((1, nst), F32)
    for c0 in range(0, S5_CHUNKS, part):
        for c in range(c0, c0 + part):
            xr = x_scr[c:c + 1, 0:nst]
            xi = x_scr[c:c + 1, nst:2 * nst]
            x_scr[c:c + 1, 0:nst] = sr
            x_scr[c:c + 1, nst:2 * nst] = si
            sr, si = at_r * sr - at_i * si + xr, at_r * si + at_i * sr + xi
        s_in = x_scr[c0:c0 + part, :].astype(BF16)
        uh = u[c0:c0 + part, :]
        for t in range(0, S5_T, 2):
            y2 = _dot(uh[:, 0:(t + 2) * LANES], k2_scr[(S5_T - 2 - t) * LANES:, :])
            y2 += _dot_nt(s_in, cyt_scr[(t + 1) * LANES:(t + 3) * LANES, :])
            for q in range(2):
                rows = pl.ds(c0 * S5_T + t + q, part, stride=S5_T)
                y_ref[0, rows, :] = y2[:, q * LANES:(q + 1) * LANES]


def _s5_core(u3, b_t, per_group, w_out, glu_w):
    blk3 = lambda *s: pl.BlockSpec((1,) + s, lambda b: (b, 0, 0))
    per_block = lambda a: pl.BlockSpec((1,) + a.shape[1:], lambda b: (b,) + (0,) * (a.ndim - 1))
    casts = [w_out, glu_w]
    return pl.pallas_call(
        _s5_kernel,
        grid=(S5_BLOCKS,),
        in_specs=[blk3(SEQ, LANES)] + [per_block(b) for b in b_t]
        + [per_block(p) for p in per_group]
        + [_slab_spec(w.shape, S5_BLOCKS) for w in casts],
        out_specs=[blk3(SEQ, LANES)] + [_slab_spec(w.shape, S5_BLOCKS) for w in casts],
        out_shape=[jax.ShapeDtypeStruct((S5_BLOCKS, SEQ, LANES), F32)]
        + [jax.ShapeDtypeStruct(w.shape, BF16) for w in casts],
        scratch_shapes=[
            pltpu.VMEM((S5_CHUNKS, S5_CHUNK_COLS), BF16),
            pltpu.VMEM((S5_CHUNK_COLS, 2 * S5_BLOCK_STATE), BF16),
            pltpu.VMEM(((S5_T + 1) * LANES, 2 * S5_BLOCK_STATE), BF16),
            pltpu.VMEM((S5_CHUNK_COLS, 2 * LANES), BF16),
            pltpu.VMEM((S5_CHUNKS, 2 * S5_BLOCK_STATE), F32),
            pltpu.VMEM((8, S5_BLOCK_STATE), F32),
        ],
        compiler_params=pltpu.CompilerParams(
            dimension_semantics=("arbitrary",), vmem_limit_bytes=VMEM_LIMIT),
        name="s5_core",
    )(u3, *b_t, *per_group, *casts)


def _out_proj_kernel(y_ref, sgu_ref, x_ref, gluw_ref, glub_ref, g_ref, wout_ref, gm_ref,
                     wdn_ref, o_ref, h_ref, wdnb_ref):
    def glu(r0):
        rs = slice(r0, r0 + SUB_ROWS)
        y = jnp.concatenate([y_ref[b, rs, :] for b in range(S5_BLOCKS)], axis=1)
        y = jax.nn.gelu(y)
        gate = jax.nn.sigmoid(_dot(y.astype(BF16), gluw_ref[...]) + glub_ref[...])
        return _rms(y * gate, g_ref[...]).astype(BF16)

    starts = list(range(0, x_ref.shape[0], SUB_ROWS))
    gated = [glu(r0) for r0 in starts]
    projs = [_dot(jnp.concatenate([a_n, sgu_ref[r0:r0 + SUB_ROWS, :]], axis=1), wout_ref[...])
             for a_n, r0 in zip(gated, starts)]
    for proj, r0 in zip(projs, starts):
        rs = slice(r0, r0 + SUB_ROWS)
        x1 = x_ref[rs, :] + proj
        o_ref[rs, :] = x1
        h_ref[rs, :] = _rms(x1, gm_ref[...]).astype(BF16)
    wdnb_ref[...] = wdn_ref[...].astype(BF16)


def _out_proj(y3, sgu_n, x, glu_w, glu_b, g, w_out, g_mlp, w_down):
    tm = ROW_TILE
    steps = SEQ // tm
    return pl.pallas_call(
        _out_proj_kernel,
        grid=(steps,),
        in_specs=[
            pl.BlockSpec((S5_BLOCKS, tm, LANES), lambda i: (0, i, 0)),
            pl.BlockSpec((tm, SGU_WIDTH), lambda i: (i, 0)),
            pl.BlockSpec((tm, D_MODEL), lambda i: (i, 0)),
            _const_spec((SSM_WIDTH, SSM_WIDTH)),
            _const_spec((1, SSM_WIDTH)),
            _const_spec((1, SSM_WIDTH)),
            _const_spec((D_MODEL, D_MODEL)),
            _const_spec((1, D_MODEL)),
            _slab_spec(w_down.shape, steps),
        ],
        out_specs=[
            pl.BlockSpec((tm, D_MODEL), lambda i: (i, 0)),
            pl.BlockSpec((tm, D_MODEL), lambda i: (i, 0)),
            _slab_spec(w_down.shape, steps),
        ],
        out_shape=[
            jax.ShapeDtypeStruct((SEQ, D_MODEL), F32),
            jax.ShapeDtypeStruct((SEQ, D_MODEL), BF16),
            jax.ShapeDtypeStruct(w_down.shape, BF16),
        ],
        compiler_params=pltpu.CompilerParams(
            dimension_semantics=("arbitrary",), vmem_limit_bytes=VMEM_LIMIT),
        name="s5_glu_out_proj",
    )(y3, sgu_n, x, glu_w, glu_b, g, w_out, g_mlp, w_down)


def _mlp_kernel(h_ref, x1_ref, wu_ref, wd_ref, gf_ref, o_ref):
    j = pl.program_id(1)
    nj = pl.num_programs(1)
    tm = o_ref.shape[0]
    xr = x1_ref.shape[0]

    def contribution(rs):
        r = jnp.square(jnp.maximum(_dot(h_ref[rs, :], wu_ref[...]), 0.0)).astype(BF16)
        return _dot(r, wd_ref[...])

    @pl.when(j == 0)
    def _():
        o_ref[...] = contribution(slice(None))
        o_ref[0:xr, :] += x1_ref[...]

    @pl.when((j > 0) & (j < nj - 1))
    def _():
        o_ref[...] += contribution(slice(None))
        rows = pl.ds(pl.multiple_of(j * xr, xr), xr)
        o_ref[rows, :] += x1_ref[...]

    @pl.when(j == nj - 1)
    def _():
        o_ref[tm - xr:tm, :] += x1_ref[...]
        starts = list(range(0, tm, SUB_ROWS))
        nxt = contribution(slice(starts[0], starts[0] + SUB_ROWS))
        for k, r0 in enumerate(starts):
            rs = slice(r0, r0 + SUB_ROWS)
            cur = nxt
            if k + 1 < len(starts):
                nxt = contribution(slice(starts[k + 1], starts[k + 1] + SUB_ROWS))
            o_ref[rs, :] = _rms(o_ref[rs, :] + cur, gf_ref[...])


def _mlp(h, x1, w_up, w_down, gf):
    tm, tf = MLP_ROW_TILE, MLP_FF_TILE
    nj = D_FF // tf
    return pl.pallas_call(
        _mlp_kernel,
        grid=(SEQ // tm, nj),
        in_specs=[
            pl.BlockSpec((tm, D_MODEL), lambda i, j: (i, 0)),
            pl.BlockSpec((tm // nj, D_MODEL), lambda i, j: (i * nj + j, 0)),
            pl.BlockSpec((D_MODEL, tf), lambda i, j: (0, j)),
            pl.BlockSpec((tf, D_MODEL), lambda i, j: (j, 0)),
            _const_spec((1, D_MODEL)),
        ],
        out_specs=pl.BlockSpec((tm, D_MODEL), lambda i, j: (i, 0)),
        out_shape=jax.ShapeDtypeStruct((SEQ, D_MODEL), F32),
        compiler_params=pltpu.CompilerParams(
            dimension_semantics=("arbitrary", "arbitrary"), vmem_limit_bytes=VMEM_LIMIT),
        name="mlp_final_norm",
    )(h, x1, w_up, w_down, gf)


def _s5_params(a_re, a_im, b_re, b_im, c_re, c_im, d, log_dt):
    nb, gpb, p, h = S5_BLOCKS, S5_GROUPS_PER_BLOCK, SSM_STATE, SSM_GROUP

    def b_layout(b):
        return jnp.transpose(b.reshape(nb, gpb, p, h), (0, 1, 3, 2))

    per_group = (a_re.reshape(nb, gpb, p), a_im.reshape(nb, gpb, p), log_dt.reshape(nb, 1, gpb),
                 d.reshape(nb, gpb, h), c_re.reshape(nb, gpb, h, p), c_im.reshape(nb, gpb, h, p))
    return (b_layout(b_re), b_layout(b_im)), per_group


def kernel(x, norm_mix_g, w_in, ssm_a_re, ssm_a_im, ssm_b_re, ssm_b_im, ssm_c_re, ssm_c_im,
           ssm_d, ssm_log_dt, ssm_glu_w, ssm_glu_b, sgu_ln_g, sgu_ln_b, sgu_w, sgu_b,
           out_norm_ssm_g, out_norm_sgu_g, w_out, norm_mlp_g, w_up, w_down, norm_final_g):
    assert norm_mix_g.shape[0] == 1, "single-layer problem: the MLP call applies the final norm"
    xs = x.reshape(SEQ, D_MODEL)
    row = lambda v: v.reshape(1, -1)
    mat = lambda w: w.reshape(w.shape[-2:])
    u3, sgu_n, w_up_bf = _in_proj(
        xs, row(norm_mix_g), mat(w_in), row(sgu_ln_g), row(sgu_ln_b),
        sgu_w.reshape(SGU_HEADS, SGU_CHUNK, SGU_CHUNK), mat(sgu_b), row(out_norm_sgu_g), mat(w_up))
    b_t, per_group = _s5_params(ssm_a_re, ssm_a_im, ssm_b_re, ssm_b_im, ssm_c_re, ssm_c_im,
                                ssm_d, ssm_log_dt)
    y3, w_out_bf, glu_w_bf = _s5_core(u3, b_t, per_group, mat(w_out), mat(ssm_glu_w))
    x1, h, w_down_bf = _out_proj(y3, sgu_n, xs, glu_w_bf, row(ssm_glu_b), row(out_norm_ssm_g),
                                 w_out_bf, row(norm_mlp_g), mat(w_down))
    out = _mlp(h, x1, w_up_bf, w_down_bf, row(norm_final_g))
    return out.reshape(x.shape)
```

```python
import jax
import jax.numpy as jnp
from jax import lax
from jax.experimental import pallas as pl
from jax.experimental.pallas import tpu as pltpu

D_MODEL = 2048
SEQ = 8192
SSM_WIDTH = 1024
SSM_GROUP = 16
SSM_STATE = 64
SGU_WIDTH = 1024
SGU_HEADS = 8
SGU_HEAD_DIM = 128
SGU_CHUNK = 128
IN_WIDTH = SSM_WIDTH + 2 * SGU_WIDTH
D_FF = 4 * D_MODEL
EPS = 1e-6

LANES = 128
S5_T = 8
S5_CHUNKS = SEQ // S5_T
S5_BLOCKS = SSM_WIDTH // LANES
S5_GROUPS_PER_BLOCK = LANES // SSM_GROUP
S5_BLOCK_STATE = S5_GROUPS_PER_BLOCK * SSM_STATE
S5_CHUNK_COLS = S5_T * LANES
S5_PARTS = 4

ROW_TILE = 512
SUB_ROWS = 256
W_IN_STAGE_ROWS = 128
W_IN_STAGE_SLOTS = 4
PIPE_DEPTH = 2
MLP_ROW_TILE = 1024
MLP_FF_TILE = 1024
V7X_VMEM_BYTES = 64 * 1024 * 1024
VMEM_LIMIT = V7X_VMEM_BYTES - 4 * 1024 * 1024

F32 = jnp.float32
BF16 = jnp.bfloat16


def _rms(x, g):
    return x * lax.rsqrt(jnp.mean(x * x, axis=-1, keepdims=True) + EPS) * g


def _const_spec(shape):
    n = len(shape)
    return pl.BlockSpec(shape, lambda *_: (0,) * n, pipeline_mode=pl.Buffered(1))


def _dot(a, b):
    return jnp.dot(a, b, preferred_element_type=F32)


def _dot_nt(a, b):
    return lax.dot_general(a, b, (((1,), (1,)), ((), ())), preferred_element_type=F32)


def _in_proj_kernel(x_ref, g_ref, win_hbm, lng_ref, lnb_ref, sw_ref, sb_ref,
                    og_ref, wup_ref, u_ref, sgu_ref, wupb_ref, y_scr, win_ref, stage, sem):
    tm = x_ref.shape[0]

    first_step = pl.program_id(0) == 0
    slots, rows = stage.shape[0], stage.shape[1]
    chunks = win_hbm.shape[0] // rows

    def chunk_copy(k):
        return pltpu.make_async_copy(
            win_hbm.at[pl.ds(k * rows, rows), :], stage.at[k % slots], sem.at[k % slots])

    @pl.when(first_step)
    def _():
        for k in range(slots - 1):
            chunk_copy(k).start(priority=k % 2)

    wupb_ref[...] = wup_ref[...].astype(BF16)

    @pl.when(first_step)
    def _():
        for k in range(chunks):
            if k + slots - 1 < chunks:
                chunk_copy(k + slots - 1).start(priority=(k + slots - 1) % 2)
            chunk_copy(k).wait()
            win_ref[k * rows:(k + 1) * rows, :] = stage[k % slots].astype(BF16)

    row = lax.broadcasted_iota(jnp.int32, (SGU_CHUNK, SGU_CHUNK), 0)
    col = lax.broadcasted_iota(jnp.int32, (SGU_CHUNK, SGU_CHUNK), 1)
    causal = row >= col
    ws = [jnp.where(causal, sw_ref[hd], 0.0).astype(BF16) for hd in range(SGU_HEADS)]
    sb_t = sb_ref[...].T

    def project(r0):
        h = _rms(x_ref[r0:r0 + SUB_ROWS, :], g_ref[...]).astype(BF16)
        zv = _dot(h, win_ref[:, SSM_WIDTH + SGU_WIDTH:])
        zu = _dot(h, win_ref[:, SSM_WIDTH:SSM_WIDTH + SGU_WIDTH])
        zs = _dot(h, win_ref[:, :SSM_WIDTH])
        return zs, zu, zv

    def mix(r0, z):
        zs, zu, zv = z
        for b in range(S5_BLOCKS):
            u_ref[b, r0:r0 + SUB_ROWS, :] = zs[:, b * LANES:(b + 1) * LANES]

        v = jax.nn.gelu(zv)
        u = jax.nn.gelu(zu)
        mu = jnp.mean(v, axis=-1, keepdims=True)
        vc = v - mu
        var = jnp.mean(vc * vc, axis=-1, keepdims=True)
        vb = (vc * lax.rsqrt(var + EPS) * lng_ref[...] + lnb_ref[...]).astype(BF16)
        for hd in range(SGU_HEADS):
            cs = slice(hd * SGU_HEAD_DIM, (hd + 1) * SGU_HEAD_DIM)
            for ck in range(SUB_ROWS // SGU_CHUNK):
                rs = slice(ck * SGU_CHUNK, (ck + 1) * SGU_CHUNK)
                ys = slice(r0 + ck * SGU_CHUNK, r0 + (ck + 1) * SGU_CHUNK)
                y_scr[ys, cs] = u[rs, cs] * (_dot(ws[hd], vb[rs, cs]) + sb_t[:, hd:hd + 1])
        sgu_ref[r0:r0 + SUB_ROWS, :] = _rms(y_scr[r0:r0 + SUB_ROWS, :], og_ref[...]).astype(BF16)

    starts = list(range(0, tm, SUB_ROWS))
    ahead = [project(r0) for r0 in starts[:PIPE_DEPTH]]
    for k, r0 in enumerate(starts):
        z = ahead.pop(0)
        if k + PIPE_DEPTH < len(starts):
            ahead.append(project(starts[k + PIPE_DEPTH]))
        mix(r0, z)


def _slab_spec(shape, steps):
    return pl.BlockSpec((shape[0] // steps, shape[1]), lambda i: (i, 0))


def _in_proj(x, g, w_in, ln_g, ln_b, sgu_w, sgu_b, og, w_up):
    tm = ROW_TILE
    steps = SEQ // tm
    casts = [w_up]
    return pl.pallas_call(
        _in_proj_kernel,
        grid=(steps,),
        in_specs=[
            pl.BlockSpec((tm, D_MODEL), lambda i: (i, 0)),
            _const_spec((1, D_MODEL)),
            pl.BlockSpec(memory_space=pl.ANY),
            _const_spec((1, SGU_WIDTH)),
            _const_spec((1, SGU_WIDTH)),
            _const_spec((SGU_HEADS, SGU_CHUNK, SGU_CHUNK)),
            _const_spec((SGU_HEADS, SGU_CHUNK)),
            _const_spec((1, SGU_WIDTH)),
        ] + [_slab_spec(w.shape, steps) for w in casts],
        out_specs=[
            pl.BlockSpec((S5_BLOCKS, tm, LANES), lambda i: (0, i, 0)),
            pl.BlockSpec((tm, SGU_WIDTH), lambda i: (i, 0)),
        ] + [_slab_spec(w.shape, steps) for w in casts],
        out_shape=[
            jax.ShapeDtypeStruct((S5_BLOCKS, SEQ, LANES), F32),
            jax.ShapeDtypeStruct((SEQ, SGU_WIDTH), BF16),
        ] + [jax.ShapeDtypeStruct(w.shape, BF16) for w in casts],
        scratch_shapes=[
            pltpu.VMEM((tm, SGU_WIDTH), F32),
            pltpu.VMEM((D_MODEL, IN_WIDTH), BF16),
            pltpu.VMEM((W_IN_STAGE_SLOTS, W_IN_STAGE_ROWS, IN_WIDTH), F32),
            pltpu.SemaphoreType.DMA((W_IN_STAGE_SLOTS,)),
        ],
        compiler_params=pltpu.CompilerParams(
            dimension_semantics=("arbitrary",), vmem_limit_bytes=VMEM_LIMIT),
        name="in_proj_sgu",
    )(x, g, w_in, ln_g, ln_b, sgu_w, sgu_b, og, *casts)


def _s5_kernel(u_ref, btr_ref, bti_ref, are_ref, aim_ref, ldt_ref, d_ref, cre_ref, cim_ref,
               wout_ref, gluw_ref,
               y_ref, woutb_ref, gluwb_ref, ucat_scr, bx_scr, cyt_scr, k2_scr, x_scr, row_scr):
    nst = S5_BLOCK_STATE
    gpb = S5_GROUPS_PER_BLOCK

    def side_by_side(rows):
        return jnp.concatenate([rows[gi:gi + 1, :] for gi in range(gpb)], axis=1)

    woutb_ref[...] = wout_ref[...].astype(BF16)
    gluwb_ref[...] = gluw_ref[...].astype(BF16)

    row_scr[0:1, :] = side_by_side(are_ref[0])
    row_scr[1:2, :] = side_by_side(aim_ref[0])
    row_scr[3:4, 0:LANES] = side_by_side(d_ref[0])
    ldt_groups = ldt_ref[0]
    col_group = lax.broadcasted_iota(jnp.int32, (1, nst), 1) >> 6
    ldt = jnp.zeros((1, nst), F32)
    for gi in range(gpb):
        ldt = jnp.where(col_group == gi, ldt_groups[:, gi:gi + 1], ldt)
    row_scr[2:3, :] = ldt
    are = row_scr[0:1, :]
    aim = row_scr[1:2, :]
    dt = jnp.exp(row_scr[2:3, :])
    d_row = row_scr[3:4, 0:LANES]
    er = jnp.exp(are * dt)
    abr = er * jnp.cos(aim * dt)
    abi = er * jnp.sin(aim * dt)
    den = are * are + aim * aim
    cr = ((abr - 1.0) * are + abi * aim) / den
    ci = (abi * are - (abr - 1.0) * aim) / den
    btr = jnp.concatenate([btr_ref[0, gi] for gi in range(gpb)], axis=1)
    bti = jnp.concatenate([bti_ref[0, gi] for gi in range(gpb)], axis=1)
    bbr = cr * btr - ci * bti
    bbi = cr * bti + ci * btr
    ctr = jnp.concatenate([cre_ref[0, gi] for gi in range(gpb)], axis=1)
    cti = jnp.concatenate([cim_ref[0, gi] for gi in range(gpb)], axis=1)

    grow = lax.broadcasted_iota(jnp.int32, (LANES, 2 * nst), 0) >> 4
    gcol = (lax.broadcasted_iota(jnp.int32, (LANES, 2 * nst), 1) & (nst - 1)) >> 6
    same = grow == gcol

    def expand(re, im):
        x = jnp.concatenate([re, im], axis=1)
        x = jnp.concatenate([x] * S5_GROUPS_PER_BLOCK, axis=0)
        return jnp.where(same, x, 0.0).astype(BF16)

    pr = jnp.ones_like(abr)
    pi = jnp.zeros_like(abr)
    for k in range(S5_T + 1):
        if k < S5_T:
            rs = slice((S5_T - 1 - k) * LANES, (S5_T - k) * LANES)
            bx_scr[rs, :] = expand(pr * bbr - pi * bbi, pr * bbi + pi * bbr)
        cyt_scr[k * LANES:(k + 1) * LANES, :] = expand(ctr * pr - cti * pi, -(ctr * pi + cti * pr))
        if k < S5_T:
            pr, pi = pr * abr - pi * abi, pr * abi + pi * abr
    at_r, at_i = pr, pi

    kst_f32 = _dot_nt(bx_scr[...], cyt_scr[0:LANES, :])
    kst = kst_f32.astype(BF16)
    eye = (lax.broadcasted_iota(jnp.int32, (LANES, LANES), 0)
           == lax.broadcasted_iota(jnp.int32, (LANES, LANES), 1))
    k0 = (kst_f32[(S5_T - 1) * LANES:, :] + jnp.where(eye, d_row, 0.0)).astype(BF16)
    k2_scr[:, LANES:2 * LANES] = kst
    k2_scr[(S5_T - 1) * LANES:, LANES:2 * LANES] = k0
    k2_scr[0:(S5_T - 1) * LANES, 0:LANES] = kst[LANES:, :]
    k2_scr[(S5_T - 2) * LANES:(S5_T - 1) * LANES, 0:LANES] = k0
    k2_scr[(S5_T - 1) * LANES:, 0:LANES] = jnp.zeros((LANES, LANES), BF16)

    for t in range(S5_T):
        ucat_scr[:, t * LANES:(t + 1) * LANES] = (
            u_ref[0, pl.ds(t, S5_CHUNKS, stride=S5_T), :].astype(BF16))
    u = ucat_scr[...]

    part = S5_CHUNKS // S5_PARTS
    for c0 in range(0, S5_CHUNKS, part):
        x_scr[c0:c0 + part, :] = _dot(u[c0:c0 + part, :], bx_scr[...])

    sr = jnp.zeros((1, nst), F32)
    si = jnp.zeros((1, nst), F32)
    for c0 in range(0, S5_CHUNKS, part):
        for c in range(c0, c0 + part):
            xr = x_scr[c:c + 1, 0:nst]
            xi = x_scr[c:c + 1, nst:2 * nst]
            x_scr[c:c + 1, 0:nst] = sr
            x_scr[c:c + 1, nst:2 * nst] = si
            sr, si = at_r * sr - at_i * si + xr, at_r * si + at_i * sr + xi
        s_in = x_scr[c0:c0 + part, :].astype(BF16)
        uh = u[c0:c0 + part, :]
        for t in range(0, S5_T, 2):
            y2 = _dot(uh[:, 0:(t + 2) * LANES], k2_scr[(S5_T - 2 - t) * LANES:, :])
            y2 += _dot_nt(s_in, cyt_scr[(t + 1) * LANES:(t + 3) * LANES, :])
            for q in range(2):
                rows = pl.ds(c0 * S5_T + t + q, part, stride=S5_T)
                y_ref[0, rows, :] = y2[:, q * LANES:(q + 1) * LANES]


def _s5_core(u3, b_t, per_group, w_out, glu_w):
    blk3 = lambda *s: pl.BlockSpec((1,) + s, lambda b: (b, 0, 0))
    per_block = lambda a: pl.BlockSpec((1,) + a.shape[1:], lambda b: (b,) + (0,) * (a.ndim - 1))
    casts = [w_out, glu_w]
    return pl.pallas_call(
        _s5_kernel,
        grid=(S5_BLOCKS,),
        in_specs=[blk3(SEQ, LANES)] + [per_block(b) for b in b_t]
        + [per_block(p) for p in per_group]
        + [_slab_spec(w.shape, S5_BLOCKS) for w in casts],
        out_specs=[blk3(SEQ, LANES)] + [_slab_spec(w.shape, S5_BLOCKS) for w in casts],
        out_shape=[jax.ShapeDtypeStruct((S5_BLOCKS, SEQ, LANES), F32)]
        + [jax.ShapeDtypeStruct(w.shape, BF16) for w in casts],
        scratch_shapes=[
            pltpu.VMEM((S5_CHUNKS, S5_CHUNK_COLS), BF16),
            pltpu.VMEM((S5_CHUNK_COLS, 2 * S5_BLOCK_STATE), BF16),
            pltpu.VMEM(((S5_T + 1) * LANES, 2 * S5_BLOCK_STATE), BF16),
            pltpu.VMEM((S5_CHUNK_COLS, 2 * LANES), BF16),
            pltpu.VMEM((S5_CHUNKS, 2 * S5_BLOCK_STATE), F32),
            pltpu.VMEM((8, S5_BLOCK_STATE), F32),
        ],
        compiler_params=pltpu.CompilerParams(
            dimension_semantics=("arbitrary",), vmem_limit_bytes=VMEM_LIMIT),
        name="s5_core",
    )(u3, *b_t, *per_group, *casts)


def _out_proj_kernel(y_ref, sgu_ref, x_ref, gluw_ref, glub_ref, g_ref, wout_ref, gm_ref,
                     wdn_ref, o_ref, h_ref, wdnb_ref):
    def glu(r0):
        rs = slice(r0, r0 + SUB_ROWS)
        y = jnp.concatenate([y_ref[b, rs, :] for b in range(S5_BLOCKS)], axis=1)
        y = jax.nn.gelu(y)
        gate = jax.nn.sigmoid(_dot(y.astype(BF16), gluw_ref[...]) + glub_ref[...])
        return _rms(y * gate, g_ref[...]).astype(BF16)

    starts = list(range(0, x_ref.shape[0], SUB_ROWS))
    gated = [glu(r0) for r0 in starts]
    projs = [_dot(jnp.concatenate([a_n, sgu_ref[r0:r0 + SUB_ROWS, :]], axis=1), wout_ref[...])
             for a_n, r0 in zip(gated, starts)]
    for proj, r0 in zip(projs, starts):
        rs = slice(r0, r0 + SUB_ROWS)
        x1 = x_ref[rs, :] + proj
        o_ref[rs, :] = x1
        h_ref[rs, :] = _rms(x1, gm_ref[...]).astype(BF16)
    wdnb_ref[...] = wdn_ref[...].astype(BF16)


def _out_proj(y3, sgu_n, x, glu_w, glu_b, g, w_out, g_mlp, w_down):
    tm = ROW_TILE
    steps = SEQ // tm
    return pl.pallas_call(
        _out_proj_kernel,
        grid=(steps,),
        in_specs=[
            pl.BlockSpec((S5_BLOCKS, tm, LANES), lambda i: (0, i, 0)),
            pl.BlockSpec((tm, SGU_WIDTH), lambda i: (i, 0)),
            pl.BlockSpec((tm, D_MODEL), lambda i: (i, 0)),
            _const_spec((SSM_WIDTH, SSM_WIDTH)),
            _const_spec((1, SSM_WIDTH)),
            _const_spec((1, SSM_WIDTH)),
            _const_spec((D_MODEL, D_MODEL)),
            _const_spec((1, D_MODEL)),
            _slab_spec(w_down.shape, steps),
        ],
        out_specs=[
            pl.BlockSpec((tm, D_MODEL), lambda i: (i, 0)),
            pl.BlockSpec((tm, D_MODEL), lambda i: (i, 0)),
            _slab_spec(w_down.shape, steps),
        ],
        out_shape=[
            jax.ShapeDtypeStruct((SEQ, D_MODEL), F32),
            jax.ShapeDtypeStruct((SEQ, D_MODEL), BF16),
            jax.ShapeDtypeStruct(w_down.shape, BF16),
        ],
        compiler_params=pltpu.CompilerParams(
            dimension_semantics=("arbitrary",), vmem_limit_bytes=VMEM_LIMIT),
        name="s5_glu_out_proj",
    )(y3, sgu_n, x, glu_w, glu_b, g, w_out, g_mlp, w_down)


def _mlp_kernel(h_ref, x1_ref, wu_ref, wd_ref, gf_ref, o_ref):
    j = pl.program_id(1)
    nj = pl.num_programs(1)
    tm = o_ref.shape[0]
    xr = x1_ref.shape[0]

    def contribution(rs):
        r = jnp.square(jnp.maximum(_dot(h_ref[rs, :], wu_ref[...]), 0.0)).astype(BF16)
        return _dot(r, wd_ref[...])

    @pl.when(j == 0)
    def _():
        o_ref[...] = contribution(slice(None))
        o_ref[0:xr, :] += x1_ref[...]

    @pl.when((j > 0) & (j < nj - 1))
    def _():
        o_ref[...] += contribution(slice(None))
        rows = pl.ds(pl.multiple_of(j * xr, xr), xr)
        o_ref[rows, :] += x1_ref[...]

    @pl.when(j == nj - 1)
    def _():
        o_ref[tm - xr:tm, :] += x1_ref[...]
        starts = list(range(0, tm, SUB_ROWS))
        nxt = contribution(slice(starts[0], starts[0] + SUB_ROWS))
        for k, r0 in enumerate(starts):
            rs = slice(r0, r0 + SUB_ROWS)
            cur = nxt
            if k + 1 < len(starts):
                nxt = contribution(slice(starts[k + 1], starts[k + 1] + SUB_ROWS))
            o_ref[rs, :] = _rms(o_ref[rs, :] + cur, gf_ref[...])


def _mlp(h, x1, w_up, w_down, gf):
    tm, tf = MLP_ROW_TILE, MLP_FF_TILE
    nj = D_FF // tf
    return pl.pallas_call(
        _mlp_kernel,
        grid=(SEQ // tm, nj),
        in_specs=[
            pl.BlockSpec((tm, D_MODEL), lambda i, j: (i, 0)),
            pl.BlockSpec((tm // nj, D_MODEL), lambda i, j: (i * nj + j, 0)),
            pl.BlockSpec((D_MODEL, tf), lambda i, j: (0, j)),
            pl.BlockSpec((tf, D_MODEL), lambda i, j: (j, 0)),
            _const_spec((1, D_MODEL)),
        ],
        out_specs=pl.BlockSpec((tm, D_MODEL), lambda i, j: (i, 0)),
        out_shape=jax.ShapeDtypeStruct((SEQ, D_MODEL), F32),
        compiler_params=pltpu.CompilerParams(
            dimension_semantics=("arbitrary", "arbitrary"), vmem_limit_bytes=VMEM_LIMIT),
        name="mlp_final_norm",
    )(h, x1, w_up, w_down, gf)


def _s5_params(a_re, a_im, b_re, b_im, c_re, c_im, d, log_dt):
    nb, gpb, p, h = S5_BLOCKS, S5_GROUPS_PER_BLOCK, SSM_STATE, SSM_GROUP

    def b_layout(b):
        return jnp.transpose(b.reshape(nb, gpb, p, h), (0, 1, 3, 2))

    per_group = (a_re.reshape(nb, gpb, p), a_im.reshape(nb, gpb, p), log_dt.reshape(nb, 1, gpb),
                 d.reshape(nb, gpb, h), c_re.reshape(nb, gpb, h, p), c_im.reshape(nb, gpb, h, p))
    return (b_layout(b_re), b_layout(b_im)), per_group


def kernel(x, norm_mix_g, w_in, ssm_a_re, ssm_a_im, ssm_b_re, ssm_b_im, ssm_c_re, ssm_c_im,
           ssm_d, ssm_log_dt, ssm_glu_w, ssm_glu_b, sgu_ln_g, sgu_ln_b, sgu_w, sgu_b,
           out_norm_ssm_g, out_norm_sgu_g, w_out, norm_mlp_g, w_up, w_down, norm_final_g):
    assert norm_mix_g.shape[0] == 1, "single-layer problem: the MLP call applies the final norm"
    xs = x.reshape(SEQ, D_MODEL)
    row = lambda v: v.reshape(1, -1)
    mat = lambda w: w.reshape(w.shape[-2:])
    u3, sgu_n, w_up_bf = _in_proj(
        xs, row(norm_mix_g), mat(w_in), row(sgu_ln_g), row(sgu_ln_b),
        sgu_w.reshape(SGU_HEADS, SGU_CHUNK, SGU_CHUNK), mat(sgu_b), row(out_norm_sgu_g), mat(w_up))
    b_t, per_group = _s5_params(ssm_a_re, ssm_a_im, ssm_b_re, ssm_b_im, ssm_c_re, ssm_c_im,
                                ssm_d, ssm_log_dt)
    y3, w_out_bf, glu_w_bf = _s5_core(u3, b_t, per_group, mat(w_out), mat(ssm_glu_w))
    x1, h, w_down_bf = _out_proj(y3, sgu_n, xs, glu_w_bf, row(ssm_glu_b), row(out_norm_ssm_g),
                                 w_out_bf, row(norm_mlp_g), mat(w_down))
    out = _mlp(h, x1, w_up_bf, w_down_bf, row(norm_final_g))
    return out.reshape(x.shape)
```
